```python
import math
import jax, jax.numpy as jnp
from jax import lax
import numpy as np

D_MODEL = 1024
BATCH = 4
SEQ = 4096
DEPTH = 1

GRID_W = 64
CTX_LEN = 256
N_HEADS = 8
QK_NOPE_DIM = 64
QK_ROPE_DIM = 32
V_HEAD_DIM = 64
Q_LORA_RANK = 256
KV_LORA_RANK = 128
ATTN_WIDTH = N_HEADS * V_HEAD_DIM
MLA_IN_WIDTH = Q_LORA_RANK + KV_LORA_RANK + QK_ROPE_DIM
ROPE_BASE = 10000.0
SOFTMAX_SCALE = (QK_NOPE_DIM + QK_ROPE_DIM) ** -0.5
Q_BLOCK = 128
SSM_WIDTH = D_MODEL - ATTN_WIDTH
SSM_GROUP = 16
N_SSM_GROUPS = SSM_WIDTH // SSM_GROUP
SSM_STATE = 64
DT_MIN = 0.001
DT_MAX = 0.1
IN_PROJ_WIDTH = MLA_IN_WIDTH + SSM_WIDTH
MIX_WIDTH = ATTN_WIDTH + SSM_WIDTH
N_EXPERT_GROUPS = 4
EXPERTS_PER_GROUP = 8
N_EXPERTS = N_EXPERT_GROUPS * EXPERTS_PER_GROUP
EXPERT_TOP_K = 2
D_FF_EXPERT = 256
ALPHA = (2 * DEPTH) ** 0.25
BETA = (8 * DEPTH) ** -0.25
EPS = 1e-6

kernel_name = "hymba_mla_s5_hmoe_deepnorm_dit"


def layer_norm(x, g, b):
    xf = x.astype(jnp.float32)
    mu = jnp.mean(xf, -1, keepdims=True)
    var = jnp.mean(jnp.square(xf - mu), -1, keepdims=True)
    return ((xf - mu) * lax.rsqrt(var + EPS) * g + b).astype(x.dtype)


def rms_norm(x, g):
    xf = x.astype(jnp.float32)
    return (xf * lax.rsqrt(jnp.mean(jnp.square(xf), -1, keepdims=True) + EPS) * g).astype(x.dtype)


def modulate(x, shift, scale):
    return x * (1.0 + scale) + shift


def axial_rope_tables(n_tok):
    rows = n_tok // GRID_W
    row = jnp.repeat(jnp.arange(rows), GRID_W).astype(jnp.float32)
    col = jnp.tile(jnp.arange(GRID_W), rows).astype(jnp.float32)
    axis_dim = QK_ROPE_DIM // 2
    inv_freq = ROPE_BASE ** (-jnp.arange(0, axis_dim, 2, dtype=jnp.float32) / axis_dim)
    ang = jnp.concatenate([row[:, None] * inv_freq, col[:, None] * inv_freq], -1)
    ang = jnp.concatenate([ang, ang], -1)
    return jnp.cos(ang), jnp.sin(ang)


def apply_rope(t, cos, sin):
    t1, t2 = jnp.split(t, 2, axis=-1)
    rot = jnp.concatenate([-t2, t1], -1)
    return (t * cos + rot * sin).astype(t.dtype)


def mla_project(p, q_norm_g, kv_norm_g, w_uq, w_ukv, rope):
    lead = p.shape[:-1]
    cq = p[..., :Q_LORA_RANK]
    ckv = p[..., Q_LORA_RANK:Q_LORA_RANK + KV_LORA_RANK]
    k_rope = p[..., Q_LORA_RANK + KV_LORA_RANK:MLA_IN_WIDTH]
    q = (rms_norm(cq, q_norm_g) @ w_uq).reshape(*lead, N_HEADS, QK_NOPE_DIM + QK_ROPE_DIM)
    kv = (rms_norm(ckv, kv_norm_g) @ w_ukv).reshape(*lead, N_HEADS, QK_NOPE_DIM + V_HEAD_DIM)
    q_nope, q_rope = q[..., :QK_NOPE_DIM], q[..., QK_NOPE_DIM:]
    k_nope, v = kv[..., :QK_NOPE_DIM], kv[..., QK_NOPE_DIM:]
    if rope is not None:
        cos, sin = rope
        q_rope = apply_rope(q_rope, cos[:, None], sin[:, None])
        k_rope = apply_rope(k_rope, cos, sin)
    return q_nope, q_rope, k_nope, k_rope, v


def mla_attend(q_nope, q_rope, k_nope, k_rope, v):
    s = (jnp.einsum('bqhd,bkhd->bhqk', q_nope, k_nope)
         + jnp.einsum('bqhr,bkr->bhqk', q_rope, k_rope))
    p = jax.nn.softmax(s.astype(jnp.float32) * SOFTMAX_SCALE, axis=-1).astype(v.dtype)
    return jnp.einsum('bhqk,bkhd->bqhd', p, v)


def latent_attention(q_nope, q_rope, k_nope, k_rope, v):
    b, n = q_nope.shape[:2]
    nblk = n // Q_BLOCK

    def to_blocks(t):
        return jnp.moveaxis(t.reshape(b, nblk, Q_BLOCK, *t.shape[2:]), 1, 0)

    out = lax.map(lambda qs: mla_attend(qs[0], qs[1], k_nope, k_rope, v),
                  (to_blocks(q_nope), to_blocks(q_rope)))
    return jnp.moveaxis(out, 0, 1).reshape(b, n, ATTN_WIDTH)


def s5_discretise(a_re, a_im, log_dt, b_re, b_im):
    f32 = jnp.float32
    a_re, a_im = a_re.astype(f32), a_im.astype(f32)
    dt = jnp.exp(log_dt.astype(f32))[:, None]
    mag = jnp.exp(a_re * dt)
    abar_re, abar_im = mag * jnp.cos(a_im * dt), mag * jnp.sin(a_im * dt)
    den = jnp.square(a_re) + jnp.square(a_im)
    num_re = abar_re - 1.0
    coef_re = (num_re * a_re + abar_im * a_im) / den
    coef_im = (abar_im * a_re - num_re * a_im) / den
    b_re, b_im = b_re.astype(f32), b_im.astype(f32)
    bbar_re = coef_re[..., None] * b_re - coef_im[..., None] * b_im
    bbar_im = coef_re[..., None] * b_im + coef_im[..., None] * b_re
    return abar_re, abar_im, bbar_re, bbar_im


def _complex_affine_combine(e1, e2):
    a1r, a1i, b1r, b1i = e1
    a2r, a2i, b2r, b2i = e2
    return (a2r * a1r - a2i * a1i, a2r * a1i + a2i * a1r,
            a2r * b1r - a2i * b1i + b2r, a2r * b1i + a2i * b1r + b2i)


def s5_states(u, s0_re, s0_im, abar_re, abar_im, bbar_re, bbar_im, reverse):
    bsz, n = u.shape[:2]
    uf = u.astype(jnp.float32).reshape(bsz, n, N_SSM_GROUPS, SSM_GROUP)
    if reverse:
        uf = uf[:, ::-1]
    bu_re = jnp.einsum('bngh,gph->bngp', uf, bbar_re)
    bu_im = jnp.einsum('bngh,gph->bngp', uf, bbar_im)
    bu_re = bu_re.at[:, 0].add(abar_re * s0_re - abar_im * s0_im)
    bu_im = bu_im.at[:, 0].add(abar_re * s0_im + abar_im * s0_re)
    a_re_b = jnp.broadcast_to(abar_re, bu_re.shape)
    a_im_b = jnp.broadcast_to(abar_im, bu_im.shape)
    _, _, st_re, st_im = lax.associative_scan(
        _complex_affine_combine, (a_re_b, a_im_b, bu_re, bu_im), axis=1)
    return st_re, st_im


def s5_readout(st_re, st_im, c_re, c_im, reverse):
    y = (jnp.einsum('bngp,ghp->bngh', st_re, c_re.astype(jnp.float32))
         - jnp.einsum('bngp,ghp->bngh', st_im, c_im.astype(jnp.float32)))
    if reverse:
        y = y[:, ::-1]
    return y.reshape(y.shape[0], y.shape[1], SSM_WIDTH)


def bidirectional_s5(u_lat, u_ctx, a_re, a_im, log_dt, b_re, b_im, c_re, c_im, with_ctx_out):
    bsz = u_lat.shape[0]
    zeros = jnp.zeros((bsz, N_SSM_GROUPS, SSM_STATE), jnp.float32)
    y_lat, y_ctx = [], []
    for d in range(2):
        reverse = d == 1
        disc = s5_discretise(a_re[d], a_im[d], log_dt[d], b_re[d], b_im[d])
        sc_re, sc_im = s5_states(u_ctx, zeros, zeros, *disc, reverse)
        sx_re, sx_im = s5_states(u_lat, sc_re[:, -1], sc_im[:, -1], *disc, reverse)
        y_lat.append(s5_readout(sx_re, sx_im, c_re[d], c_im[d], reverse))
        if with_ctx_out:
            y_ctx.append(s5_readout(sc_re, sc_im, c_re[d], c_im[d], reverse))
    return y_lat[0] + y_lat[1], (y_ctx[0] + y_ctx[1] if with_ctx_out else None)


def s5_output(y, u, d_skip, w_glu, b_glu):
    g = jax.nn.gelu(y + d_skip.astype(jnp.float32) * u.astype(jnp.float32)).astype(u.dtype)
    return g * jax.nn.sigmoid(g @ w_glu + b_glu)


def merge_groups(attn, ssm, gn_attn_g, gn_ssm_g, w_o):
    return jnp.concatenate([rms_norm(attn, gn_attn_g), rms_norm(ssm, gn_ssm_g)], -1) @ w_o


def hier_moe(h, w_rg, b_rg, w_re, b_re, w_gate, w_up, w_down):
    g_prob = jax.nn.softmax((h @ w_rg + b_rg).astype(jnp.float32), -1)
    g_w, g_idx = lax.top_k(g_prob, 1)
    e_logits = (h @ w_re + b_re).astype(jnp.float32).reshape(
        *h.shape[:-1], N_EXPERT_GROUPS, EXPERTS_PER_GROUP)
    g_sel = jax.nn.one_hot(g_idx[..., 0], N_EXPERT_GROUPS, dtype=jnp.float32)
    e_prob = jax.nn.softmax(jnp.einsum('...g,...ge->...e', g_sel, e_logits), -1)
    e_w, e_idx = lax.top_k(e_prob, EXPERT_TOP_K)
    e_w = e_w / jnp.sum(e_w, -1, keepdims=True) * g_w
    expert_id = g_idx * EXPERTS_PER_GROUP + e_idx
    combine = jnp.sum(jax.nn.one_hot(expert_id, N_EXPERTS, dtype=jnp.float32)
                      * e_w[..., None], axis=-2).astype(h.dtype)

    def per_sample(args):
        hs, cs = args
        a = jnp.einsum('nd,edf->nef', hs, w_gate)
        u = jnp.einsum('nd,edf->nef', hs, w_up)
        return jnp.einsum('nef,efd->nd', jax.nn.silu(a) * u * cs[..., None], w_down)

    return lax.map(per_sample, (h, combine))


def setup_inputs(seed: int = 0) -> dict:
    key = jax.random.key(seed)
    ks = iter(jax.random.split(key, 40))
    f32 = jnp.float32

    def nrm(shape, scale):
        return jax.random.normal(next(ks), shape, f32) * scale

    L, G, P = DEPTH, N_SSM_GROUPS, SSM_STATE
    n_idx = jnp.arange(P, dtype=f32)
    return {
        "x": nrm((BATCH, SEQ, D_MODEL), 1.0),
        "c": nrm((BATCH, D_MODEL), 1.0),
        "ctx": nrm((BATCH, CTX_LEN, D_MODEL), 1.0),
        "c_ctx": nrm((D_MODEL,), 1.0),
        "w_ada": nrm((L, D_MODEL, 6 * D_MODEL), 0.5 * D_MODEL ** -0.5),
        "b_ada": nrm((L, 6 * D_MODEL), 0.01),
        "w_in": nrm((L, D_MODEL, IN_PROJ_WIDTH), D_MODEL ** -0.5),
        "q_norm_g": 1.0 + nrm((L, Q_LORA_RANK), 0.01),
        "kv_norm_g": 1.0 + nrm((L, KV_LORA_RANK), 0.01),
        "w_uq": nrm((L, Q_LORA_RANK, N_HEADS * (QK_NOPE_DIM + QK_ROPE_DIM)), Q_LORA_RANK ** -0.5),
        "w_ukv": nrm((L, KV_LORA_RANK, N_HEADS * (QK_NOPE_DIM + V_HEAD_DIM)), KV_LORA_RANK ** -0.5),
        "ssm_a_re": -0.5 + nrm((L, 2, G, P), 0.01),
        "ssm_a_im": math.pi * n_idx + nrm((L, 2, G, P), 0.01),
        "ssm_log_dt": jax.random.uniform(next(ks), (L, 2, G), f32, math.log(DT_MIN), math.log(DT_MAX)),
        "ssm_b_re": nrm((L, 2, G, P, SSM_GROUP), (2 * SSM_GROUP) ** -0.5),
        "ssm_b_im": nrm((L, 2, G, P, SSM_GROUP), (2 * SSM_GROUP) ** -0.5),
        "ssm_c_re": nrm((L, 2, G, SSM_GROUP, P), (2 * P) ** -0.5),
        "ssm_c_im": nrm((L, 2, G, SSM_GROUP, P), (2 * P) ** -0.5),
        "ssm_d": nrm((L, SSM_WIDTH), 1.0),
        "w_glu": nrm((L, SSM_WIDTH, SSM_WIDTH), SSM_WIDTH ** -0.5),
        "b_glu": nrm((L, SSM_WIDTH), 0.01),
        "gn_attn_g": 1.0 + nrm((L, ATTN_WIDTH), 0.01),
        "gn_ssm_g": 1.0 + nrm((L, SSM_WIDTH), 0.01),
        "w_o": nrm((L, MIX_WIDTH, D_MODEL), MIX_WIDTH ** -0.5 * BETA),
        "ln1_g": 1.0 + nrm((L, D_MODEL), 0.01),
        "ln1_b": nrm((L, D_MODEL), 0.01),
        "w_router_group": nrm((L, D_MODEL, N_EXPERT_GROUPS), D_MODEL ** -0.5),
        "b_router_group": nrm((L, N_EXPERT_GROUPS), 0.01),
        "w_router_expert": nrm((L, D_MODEL, N_EXPERTS), D_MODEL ** -0.5),
        "b_router_expert": nrm((L, N_EXPERTS), 0.01),
        "w_exp_gate": nrm((L, N_EXPERTS, D_MODEL, D_FF_EXPERT), D_MODEL ** -0.5),
        "w_exp_up": nrm((L, N_EXPERTS, D_MODEL, D_FF_EXPERT), D_MODEL ** -0.5),
        "w_exp_down": nrm((L, N_EXPERTS, D_FF_EXPERT, D_MODEL), D_FF_EXPERT ** -0.5 * BETA),
        "ln2_g": 1.0 + nrm((L, D_MODEL), 0.01),
        "ln2_b": nrm((L, D_MODEL), 0.01),
    }


def reference(x, c, ctx, c_ctx, w_ada, b_ada, w_in, q_norm_g, kv_norm_g, w_uq, w_ukv,
              ssm_a_re, ssm_a_im, ssm_log_dt, ssm_b_re, ssm_b_im, ssm_c_re, ssm_c_im, ssm_d,
              w_glu, b_glu, gn_attn_g, gn_ssm_g, w_o, ln1_g, ln1_b,
              w_router_group, b_router_group, w_router_expert, b_router_expert,
              w_exp_gate, w_exp_up, w_exp_down, ln2_g, ln2_b):
    n_lat = x.shape[1]
    rope = axial_rope_tables(n_lat)
    silu_c = jax.nn.silu(c)
    silu_cc = jax.nn.silu(c_ctx)
    for l in range(DEPTH):
        last = l == DEPTH - 1
        mod_x = (silu_c @ w_ada[l] + b_ada[l])[:, None, :]
        mod_c = silu_cc @ w_ada[l] + b_ada[l]
        sh_a_x, sc_a_x, g_a_x, sh_f_x, sc_f_x, g_f_x = jnp.split(mod_x, 6, -1)
        sh_a_c, sc_a_c, g_a_c, sh_f_c, sc_f_c, g_f_c = jnp.split(mod_c, 6, -1)

        p_x = modulate(x, sh_a_x, sc_a_x) @ w_in[l]
        p_c = modulate(ctx, sh_a_c, sc_a_c) @ w_in[l]
        qx_n, qx_r, kx_n, kx_r, vx = mla_project(p_x[..., :MLA_IN_WIDTH], q_norm_g[l], kv_norm_g[l],
                                                 w_uq[l], w_ukv[l], rope)
        qc_n, qc_r, kc_n, kc_r, vc = mla_project(p_c[..., :MLA_IN_WIDTH], q_norm_g[l], kv_norm_g[l],
                                                 w_uq[l], w_ukv[l], None)
        attn_x = latent_attention(qx_n, qx_r,
                                  jnp.concatenate([kx_n, kc_n], 1),
                                  jnp.concatenate([kx_r, kc_r], 1),
                                  jnp.concatenate([vx, vc], 1))
        u_x, u_c = p_x[..., MLA_IN_WIDTH:], p_c[..., MLA_IN_WIDTH:]
        y_x, y_c = bidirectional_s5(u_x, u_c, ssm_a_re[l], ssm_a_im[l], ssm_log_dt[l],
                                    ssm_b_re[l], ssm_b_im[l], ssm_c_re[l], ssm_c_im[l],
                                    with_ctx_out=not last)
        ssm_x = s5_output(y_x, u_x, ssm_d[l], w_glu[l], b_glu[l])
        mix_x = merge_groups(attn_x, ssm_x, gn_attn_g[l], gn_ssm_g[l], w_o[l])
        x = layer_norm(ALPHA * x + g_a_x * mix_x, ln1_g[l], ln1_b[l])
        if not last:
            attn_c = mla_attend(qc_n, qc_r, kc_n, kc_r, vc).reshape(*ctx.shape[:2], ATTN_WIDTH)
            ssm_c = s5_output(y_c, u_c, ssm_d[l], w_glu[l], b_glu[l])
            mix_c = merge_groups(attn_c, ssm_c, gn_attn_g[l], gn_ssm_g[l], w_o[l])
            ctx = layer_norm(ALPHA * ctx + g_a_c * mix_c, ln1_g[l], ln1_b[l])

        ffn_x = hier_moe(modulate(x, sh_f_x, sc_f_x), w_router_group[l], b_router_group[l],
                         w_router_expert[l], b_router_expert[l],
                         w_exp_gate[l], w_exp_up[l], w_exp_down[l])
        x = layer_norm(ALPHA * x + g_f_x * ffn_x, ln2_g[l], ln2_b[l])
        if not last:
            ffn_c = hier_moe(modulate(ctx, sh_f_c, sc_f_c), w_router_group[l], b_router_group[l],
                             w_router_expert[l], b_router_expert[l],
                             w_exp_gate[l], w_exp_up[l], w_exp_down[l])
            ctx = layer_norm(ALPHA * ctx + g_f_c * ffn_c, ln2_g[l], ln2_b[l])
    return x
```

```python
import functools
import math

import jax
import jax.numpy as jnp
from jax import lax
from jax.experimental import pallas as pl
from jax.experimental.pallas import tpu as pltpu

F32 = jnp.float32
BF16 = jnp.bfloat16

D_MODEL = 1024
GRID_W = 64
N_HEADS = 8
QK_NOPE = 64
QK_ROPE = 32
V_DIM = 64
Q_RANK = 256
KV_RANK = 128
ATTN_W = N_HEADS * V_DIM
MLA_IN = Q_RANK + KV_RANK + QK_ROPE
ROPE_BASE = 10000.0
SM_SCALE = (QK_NOPE + QK_ROPE) ** -0.5
SSM_W = D_MODEL - ATTN_W
SSM_GROUP = 16
N_GROUPS = SSM_W // SSM_GROUP
SSM_STATE = 64
N_EGROUPS = 4
E_PER_GROUP = 8
N_EXPERTS = N_EGROUPS * E_PER_GROUP
D_FF = 256
DEPTH = 1
ALPHA = (2 * DEPTH) ** 0.25
EPS = 1e-6

LANES = 128
HEAD_PAD = LANES
VMEM_LIMIT = 48 * 1024 * 1024

ROW_TILE = 256
Q_TILE = 256
KV_CHUNKS = 2
SSM_CHUNK = 128
SSM_PITCH = SSM_CHUNK + 8
N_SEQ = 8
N_PAIRS = N_GROUPS // 2
MOE_TILE = 1024


def _cparams(sem):
    return pltpu.CompilerParams(dimension_semantics=sem, vmem_limit_bytes=VMEM_LIMIT)


def _full(shape):
    return pl.BlockSpec(shape, lambda *_: (0,) * len(shape))


def _adaln_kernel(c_ref, w_ref, b_ref, o_ref):
    c = c_ref[...]
    s = c * jax.nn.sigmoid(c)
    o_ref[...] = jnp.dot(s, w_ref[...], preferred_element_type=F32,
                         precision=lax.Precision.HIGHEST) + b_ref[...]


def _adaln(cvec, w_ada, b_ada):
    n_out = w_ada.shape[1]
    tn = 1536
    return pl.pallas_call(
        _adaln_kernel,
        grid=(n_out // tn,),
        in_specs=[_full(cvec.shape),
                  pl.BlockSpec((D_MODEL, tn), lambda j: (0, j)),
                  pl.BlockSpec((1, tn), lambda j: (0, j))],
        out_specs=pl.BlockSpec((cvec.shape[0], tn), lambda j: (0, j)),
        out_shape=jax.ShapeDtypeStruct((cvec.shape[0], n_out), F32),
        compiler_params=_cparams(("arbitrary",)),
        name="adaln",
    )(cvec, w_ada, b_ada.reshape(1, n_out))


def _rms(v, g):
    return v * lax.rsqrt(jnp.mean(jnp.square(v), -1, keepdims=True) + EPS) * g


def _project_kernel(with_q, mod_row_fn, x_ref, mod_ref, cos_ref, sin_ref, w_in_ref, qg_ref, kvg_ref,
                    wq_ref, wqr_ref, wk_ref, wvt_ref, e_ref, er_ref, *rest):
    if with_q:
        q_ref, k_ref, vt_ref, u_ref = rest
    else:
        k_ref, vt_ref, u_ref = rest[-3:]
    row = mod_row_fn(pl.program_id(0))
    mod = mod_ref[pl.ds(row, 1), :]
    shift = mod[:, 0:D_MODEL]
    scale = mod[:, D_MODEL:2 * D_MODEL]
    xm = (x_ref[0] * (1.0 + scale) + shift).astype(BF16)
    p = jnp.dot(xm, w_in_ref[...], preferred_element_type=F32)
    u_ref[0] = p[:, 512:1024]
    cos = cos_ref[...]
    sin = sin_ref[...]

    ckvn = _rms(p[:, Q_RANK:Q_RANK + KV_RANK], kvg_ref[...]).astype(BF16)
    kn = jnp.dot(ckvn, wk_ref[...], preferred_element_type=F32)
    kr = p[:, 384:512].astype(BF16)
    krope = (jnp.dot(kr, e_ref[...], preferred_element_type=F32) * cos
             + jnp.dot(kr, er_ref[...], preferred_element_type=F32) * sin)
    for h in range(N_HEADS):
        k_ref[0, h] = (kn[:, h * HEAD_PAD:(h + 1) * HEAD_PAD] + krope).astype(BF16)
    vt_ref[0] = lax.dot_general(wvt_ref[...], ckvn, (((1,), (1,)), ((), ())),
                                preferred_element_type=F32).astype(BF16)

    if with_q:
        qn = _rms(p[:, 0:Q_RANK], qg_ref[...]).astype(BF16)
        qa = jnp.dot(qn, wq_ref[...], preferred_element_type=F32)
        qb = jnp.dot(qn, wqr_ref[...], preferred_element_type=F32)
        for h in range(N_HEADS):
            sl = slice(h * HEAD_PAD, (h + 1) * HEAD_PAD)
            q_ref[0, h] = (qa[:, sl] * cos + qb[:, sl] * sin).astype(BF16)


def _project(x, mod, cos_t, sin_t, prep, n_all, row_off, prev=None):
    bsz, n, _ = x.shape
    with_q = prev is None
    nt = n // ROW_TILE
    off = row_off // ROW_TILE
    mod_row_fn = (lambda b: b) if with_q else (lambda b: bsz)
    w_specs = [_full(prep["w_in"].shape), _full(prep["q_g"].shape), _full(prep["kv_g"].shape),
               _full(prep["wq"].shape), _full(prep["wq_rot"].shape), _full(prep["wk"].shape),
               _full(prep["wvt"].shape), _full(prep["e"].shape), _full(prep["e_rot"].shape)]
    in_specs = [pl.BlockSpec((1, ROW_TILE, D_MODEL), lambda b, t: (b, t, 0)),
                _full(mod.shape),
                pl.BlockSpec((ROW_TILE, HEAD_PAD), lambda b, t: (t, 0)),
                pl.BlockSpec((ROW_TILE, HEAD_PAD), lambda b, t: (t, 0))] + w_specs
    k_spec = pl.BlockSpec((1, N_HEADS, ROW_TILE, HEAD_PAD), lambda b, t: (b, 0, t + off, 0))
    vt_spec = pl.BlockSpec((1, ATTN_W, ROW_TILE), lambda b, t: (b, 0, t + off))
    u_spec = pl.BlockSpec((1, ROW_TILE, SSM_W), lambda b, t: (b, t + off, 0))
    k_shape = jax.ShapeDtypeStruct((bsz, N_HEADS, n_all, HEAD_PAD), BF16)
    vt_shape = jax.ShapeDtypeStruct((bsz, ATTN_W, n_all), BF16)
    u_shape = jax.ShapeDtypeStruct((bsz, n_all, SSM_W), F32)
    args = [x, mod, cos_t, sin_t, prep["w_in"], prep["q_g"], prep["kv_g"], prep["wq"], prep["wq_rot"],
            prep["wk"], prep["wvt"], prep["e"], prep["e_rot"]]
    if with_q:
        out_specs = [pl.BlockSpec((1, N_HEADS, ROW_TILE, HEAD_PAD), lambda b, t: (b, 0, t, 0)),
                     k_spec, vt_spec, u_spec]
        out_shape = [jax.ShapeDtypeStruct((bsz, N_HEADS, n, HEAD_PAD), BF16), k_shape, vt_shape, u_shape]
        aliases = {}
    else:
        k_prev, vt_prev, u_prev = prev
        in_specs += [pl.BlockSpec(memory_space=pl.ANY)] * 3
        args += [k_prev, vt_prev, u_prev]
        out_specs = [k_spec, vt_spec, u_spec]
        out_shape = [k_shape, vt_shape, u_shape]
        aliases = {13: 0, 14: 1, 15: 2}
    return pl.pallas_call(
        functools.partial(_project_kernel, with_q, mod_row_fn), grid=(bsz, nt), in_specs=in_specs, out_specs=out_specs, out_shape=out_shape,
        input_output_aliases=aliases,
        compiler_params=_cparams(("arbitrary", "arbitrary")),
        name="project_lat" if with_q else "project_ctx",
    )(*args)


def _attn_kernel(n_chunks, q_ref, k_ref, vt_ref, o_ref, s_ref):
    kv_chunk = k_ref.shape[2] // n_chunks

    def head(h, carry):
        q = q_ref[0, h]
        m = None
        for c in range(n_chunks):
            kc = k_ref[0, h, pl.ds(c * kv_chunk, kv_chunk), :]
            s = lax.dot_general(kc, q, (((1,), (1,)), ((), ())), preferred_element_type=F32)
            s_ref[pl.ds(c * kv_chunk, kv_chunk), :] = s
            cm = jnp.max(s, axis=0, keepdims=True)
            m = cm if m is None else jnp.maximum(m, cm)
        l = jnp.zeros_like(m)
        acc = jnp.zeros((V_DIM, q.shape[0]), F32)
        for c in range(n_chunks):
            p = jnp.exp2(s_ref[pl.ds(c * kv_chunk, kv_chunk), :] - m)
            l = l + jnp.sum(p, axis=0, keepdims=True)
            vc = vt_ref[0, pl.ds(pl.multiple_of(h * V_DIM, V_DIM), V_DIM), pl.ds(c * kv_chunk, kv_chunk)]
            acc = acc + jnp.dot(vc, p.astype(BF16), preferred_element_type=F32)
        o_ref[0, pl.ds(pl.multiple_of(h * V_DIM, V_DIM), V_DIM), :] = acc / l
        return carry

    lax.fori_loop(0, N_HEADS, head, 0)


def _attention(q, k, vt):
    bsz, _, n, _ = q.shape
    n_all = k.shape[2]
    n_chunks = KV_CHUNKS
    assert n_all % (n_chunks * LANES) == 0
    return pl.pallas_call(
        functools.partial(_attn_kernel, n_chunks),
        grid=(bsz, n // Q_TILE),
        in_specs=[pl.BlockSpec((1, N_HEADS, Q_TILE, HEAD_PAD), lambda b, t: (b, 0, t, 0)),
                  pl.BlockSpec((1, N_HEADS, n_all, HEAD_PAD), lambda b, t: (b, 0, 0, 0)),
                  pl.BlockSpec((1, ATTN_W, n_all), lambda b, t: (b, 0, 0))],
        out_specs=pl.BlockSpec((1, ATTN_W, Q_TILE), lambda b, t: (b, 0, t)),
        out_shape=jax.ShapeDtypeStruct((bsz, ATTN_W, n), F32),
        scratch_shapes=[pltpu.VMEM((n_all, Q_TILE), F32)],
        compiler_params=_cparams(("arbitrary", "arbitrary")),
        name="attention",
    )(q, k, vt)


def _ssm_kernel(bsz, u0_ref, u1_ref, wb_ref, wc_ref, are_ref, aim_ref, j_ref, y0_ref, y1_ref,
                bu_ref, st_ref):
    tc = SSM_CHUNK
    i = pl.program_id(0)

    @pl.when(i == 0)
    def _():
        st_ref[...] = jnp.zeros_like(st_ref)

    jrev = j_ref[...]
    for d in range(2):
        u_ref = u0_ref if d == 0 else u1_ref
        ub = u_ref[...].astype(BF16)
        if d == 1:
            ub = jnp.stack([jnp.dot(jrev, ub[b], preferred_element_type=F32).astype(BF16)
                            for b in range(bsz)])
        ub = ub.reshape(bsz * tc, SSM_W)
        for j in range(N_PAIRS):
            kb = (j // 4) * LANES
            r = jnp.dot(ub[:, kb:kb + LANES], wb_ref[d, j], preferred_element_type=F32)
            for b in range(bsz):
                row = (d * bsz + b) * SSM_PITCH
                bu_ref[2 * j, row:row + tc, :] = r[b * tc:(b + 1) * tc, 0:LANES]
                bu_ref[2 * j + 1, row:row + tc, :] = r[b * tc:(b + 1) * tc, LANES:2 * LANES]

    group = 4
    for j0 in range(0, N_PAIRS, group):
        are = [are_ref[j0 + g] for g in range(group)]
        aim = [aim_ref[j0 + g] for g in range(group)]
        init = tuple(st_ref[2 * (j0 + g) + c] for g in range(group) for c in range(2))

        def step(t, carry, j0=j0, are=are, aim=aim):
            out = []
            for g in range(group):
                sre, sim = carry[2 * g], carry[2 * g + 1]
                bre = bu_ref[2 * (j0 + g), pl.ds(t, N_SEQ, stride=SSM_PITCH), :]
                bim = bu_ref[2 * (j0 + g) + 1, pl.ds(t, N_SEQ, stride=SSM_PITCH), :]
                nre = are[g] * sre - aim[g] * sim + bre
                nim = are[g] * sim + aim[g] * sre + bim
                bu_ref[2 * (j0 + g), pl.ds(t, N_SEQ, stride=SSM_PITCH), :] = nre
                bu_ref[2 * (j0 + g) + 1, pl.ds(t, N_SEQ, stride=SSM_PITCH), :] = nim
                out += [nre, nim]
            return tuple(out)

        fin = lax.fori_loop(0, tc, step, init, unroll=2)
        for g in range(group):
            st_ref[2 * (j0 + g)] = fin[2 * g]
            st_ref[2 * (j0 + g) + 1] = fin[2 * g + 1]

    for d in range(2):
        y_ref = y0_ref if d == 0 else y1_ref
        for b in range(bsz):
            row = (d * bsz + b) * SSM_PITCH
            blocks = []
            for m in range(SSM_W // LANES):
                acc = None
                for j in range(4 * m, 4 * m + 4):
                    s = jnp.concatenate([bu_ref[2 * j, row:row + tc, :], bu_ref[2 * j + 1, row:row + tc, :]],
                                        axis=-1).astype(BF16)
                    t = jnp.dot(s, wc_ref[d, j], preferred_element_type=F32)
                    acc = t if acc is None else acc + t
                blocks.append(acc)
            y = jnp.concatenate(blocks, axis=-1)
            if d == 1:
                hi = y.astype(BF16)
                lo = (y - hi.astype(F32)).astype(BF16)
                y = (jnp.dot(jrev, hi, preferred_element_type=F32)
                     + jnp.dot(jrev, lo, preferred_element_type=F32))
            y_ref[b] = y


def _ssm(u_all, ssm):
    bsz, n_all, _ = u_all.shape
    assert N_SEQ == 2 * bsz
    nc = n_all // SSM_CHUNK
    ctx_chunks = 256 // SSM_CHUNK

    def rev_map(i):
        return jnp.where(i < ctx_chunks, ctx_chunks - 1 - i, nc - 1 - (i - ctx_chunks))

    blk = (bsz, SSM_CHUNK, SSM_W)
    return pl.pallas_call(
        functools.partial(_ssm_kernel, bsz),
        grid=(nc,),
        in_specs=[pl.BlockSpec(blk, lambda i: (0, i, 0)),
                  pl.BlockSpec(blk, lambda i: (0, rev_map(i), 0)),
                  _full(ssm["wb"].shape), _full(ssm["wc"].shape),
                  _full(ssm["a_re"].shape), _full(ssm["a_im"].shape), _full(ssm["jrev"].shape)],
        out_specs=[pl.BlockSpec(blk, lambda i: (0, i, 0)),
                   pl.BlockSpec(blk, lambda i: (0, rev_map(i), 0))],
        out_shape=[jax.ShapeDtypeStruct(u_all.shape, F32)] * 2,
        scratch_shapes=[pltpu.VMEM((2 * N_PAIRS, N_SEQ * SSM_PITCH, LANES), F32),
                        pltpu.VMEM((2 * N_PAIRS, N_SEQ, LANES), F32)],
        compiler_params=_cparams(("arbitrary",)),
        name="ssm",
    )(u_all, u_all, ssm["wb"], ssm["wc"], ssm["a_re"], ssm["a_im"], ssm["jrev"])


def _layer_norm(v, g, b):
    mu = jnp.mean(v, -1, keepdims=True)
    var = jnp.mean(jnp.square(v - mu), -1, keepdims=True)
    return (v - mu) * lax.rsqrt(var + EPS) * g + b


def _gelu_tanh(v):
    return 0.5 * v * (1.0 + jnp.tanh(math.sqrt(2.0 / math.pi) * (v + 0.044715 * (v * v * v))))


def _route(logits):
    lane = lax.broadcasted_iota(jnp.int32, logits.shape, 1).astype(F32)
    neg = jnp.float32(-jnp.inf)
    big = jnp.float32(1 << 20)
    gl = jnp.where((lane >= N_EXPERTS) & (lane < N_EXPERTS + N_EGROUPS), logits, neg)
    gmax = jnp.max(gl, -1, keepdims=True)
    g_w = 1.0 / jnp.sum(jnp.exp(gl - gmax), -1, keepdims=True)
    g_lane = jnp.min(jnp.where(gl == gmax, lane, big), -1, keepdims=True)
    lo = (g_lane - N_EXPERTS) * E_PER_GROUP
    el = jnp.where((lane >= lo) & (lane < lo + E_PER_GROUP), logits, neg)
    m1 = jnp.max(el, -1, keepdims=True)
    i1 = jnp.min(jnp.where(el == m1, lane, big), -1, keepdims=True)
    el2 = jnp.where(lane == i1, neg, el)
    m2 = jnp.max(el2, -1, keepdims=True)
    i2 = jnp.min(jnp.where(el2 == m2, lane, big), -1, keepdims=True)
    esum = jnp.sum(jnp.exp(el - m1), -1, keepdims=True)
    p1 = 1.0 / esum
    p2 = jnp.exp(m2 - m1) / esum
    w1 = p1 / (p1 + p2) * g_w
    w2 = p2 / (p1 + p2) * g_w
    return jnp.where(lane == i1, w1, 0.0) + jnp.where(lane == i2, w2, 0.0)


def _merge_kernel(at_ref, y0_ref, y1_ref, u_ref, x_ref, mod_ref, dsk_ref, wglu_ref, bglu_ref, gna_ref, gns_ref,
                  wo_ref, ln1g_ref, ln1b_ref, wr_ref, br_ref, x1_ref, h_ref, comb_ref):
    b = pl.program_id(0)
    mod = mod_ref[pl.ds(b, 1), :]
    gate_a = mod[:, 2 * D_MODEL:3 * D_MODEL]
    shift_f = mod[:, 3 * D_MODEL:4 * D_MODEL]
    scale_f = mod[:, 4 * D_MODEL:5 * D_MODEL]

    g = _gelu_tanh(y0_ref[0] + y1_ref[0] + dsk_ref[...] * u_ref[0])
    z = jnp.dot(g.astype(BF16), wglu_ref[...], preferred_element_type=F32) + bglu_ref[...]
    ssm = g * jax.nn.sigmoid(z)
    attn = at_ref[0].T
    an = _rms(attn, gna_ref[...]).astype(BF16)
    sn = _rms(ssm, gns_ref[...]).astype(BF16)
    mix = (jnp.dot(an, wo_ref[0:ATTN_W, :], preferred_element_type=F32)
           + jnp.dot(sn, wo_ref[ATTN_W:ATTN_W + SSM_W, :], preferred_element_type=F32))
    x1 = _layer_norm(ALPHA * x_ref[0] + gate_a * mix, ln1g_ref[...], ln1b_ref[...])
    x1_ref[0] = x1
    h = x1 * (1.0 + scale_f) + shift_f
    h_ref[0] = h.astype(BF16)
    logits = jnp.dot(h, wr_ref[...], preferred_element_type=F32, precision=lax.Precision.HIGHEST) + br_ref[...]
    comb_ref[0] = _route(logits)


def _merge(attn_t, y0, y1, u_all, x, mod, prep, ctx_len):
    bsz, n, _ = x.shape
    off = ctx_len // ROW_TILE
    tok = lambda w: pl.BlockSpec((1, ROW_TILE, w), lambda b, t: (b, t, 0))
    tok_off = lambda w: pl.BlockSpec((1, ROW_TILE, w), lambda b, t: (b, t + off, 0))
    weights = [prep["d_skip"], prep["w_glu"], prep["b_glu"], prep["gn_attn"], prep["gn_ssm"], prep["w_o"],
               prep["ln1_g"], prep["ln1_b"], prep["w_router"], prep["b_router"]]
    return pl.pallas_call(
        _merge_kernel,
        grid=(bsz, n // ROW_TILE),
        in_specs=[pl.BlockSpec((1, ATTN_W, ROW_TILE), lambda b, t: (b, 0, t)),
                  tok_off(SSM_W), tok_off(SSM_W), tok_off(SSM_W), tok(D_MODEL), _full(mod.shape)]
                 + [_full(w.shape) for w in weights],
        out_specs=[tok(D_MODEL), tok(D_MODEL), tok(LANES)],
        out_shape=[jax.ShapeDtypeStruct((bsz, n, D_MODEL), F32),
                   jax.ShapeDtypeStruct((bsz, n, D_MODEL), BF16),
                   jax.ShapeDtypeStruct((bsz, n, LANES), F32)],
        compiler_params=_cparams(("arbitrary", "arbitrary")),
        name="merge",
    )(attn_t, y0, y1, u_all, x, mod, *weights)


def _moe_kernel(tiles_per_batch, h_ref, comb_ref, x1_ref, mod_ref, wg_ref, wu_ref, wd_ref, ln2g_ref, ln2b_ref,
                o_ref, acc_ref):
    e = pl.program_id(1)

    @pl.when(e == 0)
    def _():
        acc_ref[...] = jnp.zeros_like(acc_ref)

    h = h_ref[...]
    a = jnp.dot(h, wg_ref[0], preferred_element_type=F32)
    u = jnp.dot(h, wu_ref[0], preferred_element_type=F32)
    comb = comb_ref[...]
    lane = lax.broadcasted_iota(jnp.int32, comb.shape, 1)
    cw =jnp.sum(jnp.where(lane == e, comb, 0.0), -1, keepdims=True)
    act = (a * jax.nn.sigmoid(a) * u * cw).astype(BF16)
    acc_ref[...] += jnp.dot(act, wd_ref[0], preferred_element_type=F32)

    @pl.when(e == N_EXPERTS - 1)
    def _():
        b = pl.program_id(0) // tiles_per_batch
        gate_f = mod_ref[pl.ds(b, 1), 5 * D_MODEL:6 * D_MODEL]
        o_ref[...] = _layer_norm(ALPHA * x1_ref[...] + gate_f * acc_ref[...], ln2g_ref[...], ln2b_ref[...])


def _moe(h, comb, x1, mod, wg, wu, wd, ln2_g, ln2_b, n):
    rows = h.shape[0]
    tile = min(MOE_TILE, n)
    tok = lambda w: pl.BlockSpec((tile, w), lambda t, e: (t, 0))
    return pl.pallas_call(
        functools.partial(_moe_kernel, n // tile),
        grid=(rows // tile, N_EXPERTS),
        in_specs=[tok(D_MODEL), tok(LANES), tok(D_MODEL), _full(mod.shape),
                  pl.BlockSpec((1, D_MODEL, D_FF), lambda t, e: (e, 0, 0)),
                  pl.BlockSpec((1, D_MODEL, D_FF), lambda t, e: (e, 0, 0)),
                  pl.BlockSpec((1, D_FF, D_MODEL), lambda t, e: (e, 0, 0)),
                  _full(ln2_g.shape), _full(ln2_b.shape)],
        out_specs=tok(D_MODEL),
        out_shape=jax.ShapeDtypeStruct((rows, D_MODEL), F32),
        scratch_shapes=[pltpu.VMEM((tile, D_MODEL), F32)],
        compiler_params=_cparams(("arbitrary", "arbitrary")),
        name="moe",
    )(h, comb, x1, mod, wg, wu, wd, ln2_g, ln2_b)


def _rope_tables(n):
    rows = n // GRID_W
    row = jnp.repeat(jnp.arange(rows), GRID_W).astype(F32)
    col = jnp.tile(jnp.arange(GRID_W), rows).astype(F32)
    axis_dim = QK_ROPE // 2
    inv_freq = ROPE_BASE ** (-jnp.arange(0, axis_dim, 2, dtype=F32) / axis_dim)
    ang = jnp.concatenate([row[:, None] * inv_freq, col[:, None] * inv_freq], -1)
    ang = jnp.concatenate([ang, ang], -1)
    pad_l = jnp.ones((n, QK_NOPE), F32)
    pad_r = jnp.ones((n, HEAD_PAD - QK_NOPE - QK_ROPE), F32)
    cos_t = jnp.concatenate([pad_l, jnp.cos(ang), pad_r], -1)
    sin_t = jnp.concatenate([0 * pad_l, jnp.sin(ang), 0 * pad_r], -1)
    return cos_t, sin_t


def _rot_half_cols(w):
    w1, w2 = w[..., :QK_ROPE // 2], w[..., QK_ROPE // 2:]
    return jnp.concatenate([-w2, w1], -1)


def _prep_project(w_in, q_norm_g, kv_norm_g, w_uq, w_ukv):
    zeros = lambda r, c: jnp.zeros((r, c), F32)
    w_in_p = jnp.concatenate([w_in[:, :MLA_IN], zeros(D_MODEL, 512 - MLA_IN), w_in[:, MLA_IN:]], -1)
    qk_scale = SM_SCALE * math.log2(math.e)
    wq = (w_uq * qk_scale).reshape(Q_RANK, N_HEADS, QK_NOPE + QK_ROPE)
    pad = jnp.zeros((Q_RANK, N_HEADS, HEAD_PAD - QK_NOPE - QK_ROPE), F32)
    wq_pad = jnp.concatenate([wq, pad], -1).reshape(Q_RANK, N_HEADS * HEAD_PAD)
    wq_rot = jnp.concatenate([0 * wq[..., :QK_NOPE], _rot_half_cols(wq[..., QK_NOPE:]), pad], -1)
    wq_rot = wq_rot.reshape(Q_RANK, N_HEADS * HEAD_PAD)
    wkv = w_ukv.reshape(KV_RANK, N_HEADS, QK_NOPE + V_DIM)
    wk = jnp.concatenate([wkv[..., :QK_NOPE], jnp.zeros((KV_RANK, N_HEADS, HEAD_PAD - QK_NOPE), F32)], -1)
    wk = wk.reshape(KV_RANK, N_HEADS * HEAD_PAD)
    wvt = wkv[..., QK_NOPE:].reshape(KV_RANK, ATTN_W).T
    eye = jnp.eye(QK_ROPE, dtype=F32)
    place = lambda m: jnp.zeros((LANES, HEAD_PAD), F32).at[:QK_ROPE, QK_NOPE:QK_NOPE + QK_ROPE].set(m)
    return dict(w_in=w_in_p.astype(BF16), q_g=q_norm_g.reshape(1, -1), kv_g=kv_norm_g.reshape(1, -1),
                wq=wq_pad.astype(BF16), wq_rot=wq_rot.astype(BF16), wk=wk.astype(BF16),
                wvt=wvt.astype(BF16), e=place(eye).astype(BF16), e_rot=place(_rot_half_cols(eye)).astype(BF16))


def _discretise(a_re, a_im, log_dt, b_re, b_im):
    dt = jnp.exp(log_dt)[:, None]
    mag = jnp.exp(a_re * dt)
    abar_re, abar_im = mag * jnp.cos(a_im * dt), mag * jnp.sin(a_im * dt)
    den = jnp.square(a_re) + jnp.square(a_im)
    num_re = abar_re - 1.0
    coef_re = (num_re * a_re + abar_im * a_im) / den
    coef_im = (abar_im * a_re - num_re * a_im) / den
    bbar_re = coef_re[..., None] * b_re - coef_im[..., None] * b_im
    bbar_im = coef_re[..., None] * b_im + coef_im[..., None] * b_re
    return abar_re, abar_im, bbar_re, bbar_im


def _prep_ssm(a_re, a_im, log_dt, b_re, b_im, c_re, c_im, bsz):
    wbs, wcs, ares, aims = [], [], [], []
    eye2 = jnp.eye(2, dtype=F32)
    for d in range(2):
        abar_re, abar_im, bbar_re, bbar_im = _discretise(a_re[d], a_im[d], log_dt[d], b_re[d], b_im[d])

        def in_block(bb):
            t = bb.reshape(N_PAIRS, 2, SSM_STATE, SSM_GROUP)
            return jnp.einsum('jgph,gk->jghkp', t, eye2).reshape(N_PAIRS, 2 * SSM_GROUP, 2 * SSM_STATE)

        blk = jnp.concatenate([in_block(bbar_re), in_block(bbar_im)], -1)
        wb = jnp.zeros((N_PAIRS, LANES, 2 * LANES), F32)
        for j in range(N_PAIRS):
            r0 = 32 * (j % 4)
            wb = wb.at[j, r0:r0 + 32, :].set(blk[j])
        wbs.append(wb)

        def out_block(cc):
            t = cc.reshape(N_PAIRS, 2, SSM_GROUP, SSM_STATE)
            return jnp.einsum('jghp,gk->jgpkh', t, eye2).reshape(N_PAIRS, 2 * SSM_STATE, 2 * SSM_GROUP)

        oblk = jnp.concatenate([out_block(c_re[d]), -out_block(c_im[d])], 1)
        wc = jnp.zeros((N_PAIRS, 2 * LANES, LANES), F32)
        for j in range(N_PAIRS):
            c0 = 32 * (j % 4)
            wc = wc.at[j, :, c0:c0 + 32].set(oblk[j])
        wcs.append(wc)
        ares.append(jnp.broadcast_to(abar_re.reshape(N_PAIRS, 1, LANES), (N_PAIRS, bsz, LANES)))
        aims.append(jnp.broadcast_to(abar_im.reshape(N_PAIRS, 1, LANES), (N_PAIRS, bsz, LANES)))
    jrev = jnp.eye(SSM_CHUNK, dtype=F32)[::-1]
    return dict(wb=jnp.stack(wbs).astype(BF16), wc=jnp.stack(wcs).astype(BF16),
                a_re=jnp.concatenate(ares, 1), a_im=jnp.concatenate(aims, 1), jrev=jrev.astype(BF16))


def kernel(x, c, ctx, c_ctx, w_ada, b_ada, w_in, q_norm_g, kv_norm_g, w_uq, w_ukv, ssm_a_re, ssm_a_im,
           ssm_log_dt, ssm_b_re, ssm_b_im, ssm_c_re, ssm_c_im, ssm_d, w_glu, b_glu, gn_attn_g, gn_ssm_g, w_o,
           ln1_g, ln1_b, w_router_group, b_router_group, w_router_expert, b_router_expert, w_exp_gate,
           w_exp_up, w_exp_down, ln2_g, ln2_b):
    bsz, n, _ = x.shape
    ctx_len = ctx.shape[1]
    n_all = ctx_len + n
    l = 0
    row = lambda v: v.reshape(1, -1)

    cvec = jnp.concatenate([c, c_ctx[None, :], jnp.zeros((8 - bsz - 1, D_MODEL), F32)], 0)
    mod = _adaln(cvec, w_ada[l], b_ada[l])

    prep = _prep_project(w_in[l], q_norm_g[l], kv_norm_g[l], w_uq[l], w_ukv[l])
    cos_t, sin_t = _rope_tables(n)
    ones_t = jnp.ones((ctx_len, HEAD_PAD), F32)
    q, k, vt, u_all = _project(x, mod, cos_t, sin_t, prep, n_all, ctx_len)
    k, vt, u_all = _project(ctx, mod, ones_t, 0 * ones_t, prep, n_all, 0, prev=(k, vt, u_all))

    attn_t = _attention(q, k, vt)

    ssm = _prep_ssm(ssm_a_re[l], ssm_a_im[l], ssm_log_dt[l], ssm_b_re[l], ssm_b_im[l],
                    ssm_c_re[l], ssm_c_im[l], bsz)
    y0, y1 = _ssm(u_all, ssm)

    w_router = jnp.concatenate([w_router_expert[l], w_router_group[l],
                                jnp.zeros((D_MODEL, LANES - N_EXPERTS - N_EGROUPS), F32)], -1)
    b_router = jnp.concatenate([b_router_expert[l], b_router_group[l],
                                jnp.zeros((LANES - N_EXPERTS - N_EGROUPS,), F32)])
    mprep = dict(d_skip=row(ssm_d[l]), w_glu=w_glu[l].astype(BF16), b_glu=row(b_glu[l]),
                 gn_attn=row(gn_attn_g[l]), gn_ssm=row(gn_ssm_g[l]), w_o=w_o[l].astype(BF16),
                 ln1_g=row(ln1_g[l]), ln1_b=row(ln1_b[l]), w_router=w_router, b_router=row(b_router))
    x1, h, comb = _merge(attn_t, y0, y1, u_all, x, mod, mprep, ctx_len)

    out = _moe(h.reshape(bsz * n, D_MODEL), comb.reshape(bsz * n, LANES), x1.reshape(bsz * n, D_MODEL), mod,
               w_exp_gate[l].astype(BF16), w_exp_up[l].astype(BF16), w_exp_down[l].astype(BF16),
               row(ln2_g[l]), row(ln2_b[l]), n)
    return out.reshape(bsz, n, D_MODEL)
```

```python
import functools
import math

import jax
import jax.numpy as jnp
from jax import lax
from jax.experimental import pallas as pl
from jax.experimental.pallas import tpu as pltpu

F32 = jnp.float32
BF16 = jnp.bfloat16

D_MODEL = 1024
GRID_W = 64
N_HEADS = 8
QK_NOPE = 64
QK_ROPE = 32
V_DIM = 64
Q_RANK = 256
KV_RANK = 128
ATTN_W = N_HEADS * V_DIM
MLA_IN = Q_RANK + KV_RANK + QK_ROPE
ROPE_BASE = 10000.0
SM_SCALE = (QK_NOPE + QK_ROPE) ** -0.5
SSM_W = D_MODEL - ATTN_W
SSM_GROUP = 16
N_GROUPS = SSM_W // SSM_GROUP
SSM_STATE = 64
N_EGROUPS = 4
E_PER_GROUP = 8
N_EXPERTS = N_EGROUPS * E_PER_GROUP
D_FF = 256
DEPTH = 1
ALPHA = (2 * DEPTH) ** 0.25
EPS = 1e-6

LANES = 128
HEAD_PAD = LANES
VMEM_LIMIT = 48 * 1024 * 1024

ROW_TILE = 256
Q_TILE = 256
KV_CHUNKS = 2
SSM_CHUNK = 128
SSM_PITCH = SSM_CHUNK + 8
N_SEQ = 8
N_PAIRS = N_GROUPS // 2
RANK_TILE = 512
EXPERT_TILE = 256
SLAB_ROWS = D_MODEL // LANES
XBUF_PITCH = EXPERT_TILE + 8
EXPERT_VMEM_LIMIT = 56 * 1024 * 1024


def _cparams(sem):
    return pltpu.CompilerParams(dimension_semantics=sem, vmem_limit_bytes=VMEM_LIMIT)


def _full(shape):
    return pl.BlockSpec(shape, lambda *_: (0,) * len(shape))


def _adaln_kernel(c_ref, w_ref, b_ref, o_ref):
    c = c_ref[...]
    s = c * jax.nn.sigmoid(c)
    o_ref[...] = jnp.dot(s, w_ref[...], preferred_element_type=F32,
                         precision=lax.Precision.HIGHEST) + b_ref[...]


def _adaln(cvec, w_ada, b_ada):
    n_out = w_ada.shape[1]
    tn = 1536
    return pl.pallas_call(
        _adaln_kernel,
        grid=(n_out // tn,),
        in_specs=[_full(cvec.shape),
                  pl.BlockSpec((D_MODEL, tn), lambda j: (0, j)),
                  pl.BlockSpec((1, tn), lambda j: (0, j))],
        out_specs=pl.BlockSpec((cvec.shape[0], tn), lambda j: (0, j)),
        out_shape=jax.ShapeDtypeStruct((cvec.shape[0], n_out), F32),
        compiler_params=_cparams(("arbitrary",)),
        name="adaln",
    )(cvec, w_ada, b_ada.reshape(1, n_out))


def _rms(v, g):
    return v * lax.rsqrt(jnp.mean(jnp.square(v), -1, keepdims=True) + EPS) * g


def _project_kernel(with_q, mod_row_fn, x_ref, mod_ref, cos_ref, sin_ref, w_in_ref, qg_ref, kvg_ref,
                    wq_ref, wqr_ref, wk_ref, wvt_ref, e_ref, er_ref, *rest):
    if with_q:
        q_ref, k_ref, vt_ref, u_ref = rest
    else:
        k_ref, vt_ref, u_ref = rest[-3:]
    row = mod_row_fn(pl.program_id(0))
    mod = mod_ref[pl.ds(row, 1), :]
    shift = mod[:, 0:D_MODEL]
    scale = mod[:, D_MODEL:2 * D_MODEL]
    xm = (x_ref[0] * (1.0 + scale) + shift).astype(BF16)
    p = jnp.dot(xm, w_in_ref[...], preferred_element_type=F32)
    u_ref[0] = p[:, 512:1024]
    cos = cos_ref[...]
    sin = sin_ref[...]

    ckvn = _rms(p[:, Q_RANK:Q_RANK + KV_RANK], kvg_ref[...]).astype(BF16)
    kn = jnp.dot(ckvn, wk_ref[...], preferred_element_type=F32)
    kr = p[:, 384:512].astype(BF16)
    krope = (jnp.dot(kr, e_ref[...], preferred_element_type=F32) * cos
             + jnp.dot(kr, er_ref[...], preferred_element_type=F32) * sin)
    for h in range(N_HEADS):
        k_ref[0, h] = (kn[:, h * HEAD_PAD:(h + 1) * HEAD_PAD] + krope).astype(BF16)
    vt_ref[0] = lax.dot_general(wvt_ref[...], ckvn, (((1,), (1,)), ((), ())),
                                preferred_element_type=F32).astype(BF16)

    if with_q:
        qn = _rms(p[:, 0:Q_RANK], qg_ref[...]).astype(BF16)
        qa = jnp.dot(qn, wq_ref[...], preferred_element_type=F32)
        qb = jnp.dot(qn, wqr_ref[...], preferred_element_type=F32)
        for h in range(N_HEADS):
            sl = slice(h * HEAD_PAD, (h + 1) * HEAD_PAD)
            q_ref[0, h] = (qa[:, sl] * cos + qb[:, sl] * sin).astype(BF16)


def _project(x, mod, cos_t, sin_t, prep, n_all, row_off, prev=None):
    bsz, n, _ = x.shape
    with_q = prev is None
    nt = n // ROW_TILE
    off = row_off // ROW_TILE
    mod_row_fn = (lambda b: b) if with_q else (lambda b: bsz)
    w_specs = [_full(prep["w_in"].shape), _full(prep["q_g"].shape), _full(prep["kv_g"].shape),
               _full(prep["wq"].shape), _full(prep["wq_rot"].shape), _full(prep["wk"].shape),
               _full(prep["wvt"].shape), _full(prep["e"].shape), _full(prep["e_rot"].shape)]
    in_specs = [pl.BlockSpec((1, ROW_TILE, D_MODEL), lambda b, t: (b, t, 0)),
                _full(mod.shape),
                pl.BlockSpec((ROW_TILE, HEAD_PAD), lambda b, t: (t, 0)),
                pl.BlockSpec((ROW_TILE, HEAD_PAD), lambda b, t: (t, 0))] + w_specs
    k_spec = pl.BlockSpec((1, N_HEADS, ROW_TILE, HEAD_PAD), lambda b, t: (b, 0, t + off, 0))
    vt_spec = pl.BlockSpec((1, ATTN_W, ROW_TILE), lambda b, t: (b, 0, t + off))
    u_spec = pl.BlockSpec((1, ROW_TILE, SSM_W), lambda b, t: (b, t + off, 0))
    k_shape = jax.ShapeDtypeStruct((bsz, N_HEADS, n_all, HEAD_PAD), BF16)
    vt_shape = jax.ShapeDtypeStruct((bsz, ATTN_W, n_all), BF16)
    u_shape = jax.ShapeDtypeStruct((bsz, n_all, SSM_W), F32)
    args = [x, mod, cos_t, sin_t, prep["w_in"], prep["q_g"], prep["kv_g"], prep["wq"], prep["wq_rot"],
            prep["wk"], prep["wvt"], prep["e"], prep["e_rot"]]
    if with_q:
        out_specs = [pl.BlockSpec((1, N_HEADS, ROW_TILE, HEAD_PAD), lambda b, t: (b, 0, t, 0)),
                     k_spec, vt_spec, u_spec]
        out_shape = [jax.ShapeDtypeStruct((bsz, N_HEADS, n, HEAD_PAD), BF16), k_shape, vt_shape, u_shape]
        aliases = {}
    else:
        k_prev, vt_prev, u_prev = prev
        in_specs += [pl.BlockSpec(memory_space=pl.ANY)] * 3
        args += [k_prev, vt_prev, u_prev]
        out_specs = [k_spec, vt_spec, u_spec]
        out_shape = [k_shape, vt_shape, u_shape]
        aliases = {13: 0, 14: 1, 15: 2}
    return pl.pallas_call(
        functools.partial(_project_kernel, with_q, mod_row_fn), grid=(bsz, nt), in_specs=in_specs, out_specs=out_specs, out_shape=out_shape,
        input_output_aliases=aliases,
        compiler_params=_cparams(("arbitrary", "arbitrary")),
        name="project_lat" if with_q else "project_ctx",
    )(*args)


def _attn_kernel(n_chunks, q_ref, k_ref, vt_ref, o_ref, sa_ref, sb_ref):
    kv_chunk = k_ref.shape[2] // n_chunks

    def scores(h, s_ref):
        q = q_ref[0, h]
        m = None
        for c in range(n_chunks):
            kc = k_ref[0, h, pl.ds(c * kv_chunk, kv_chunk), :]
            s = lax.dot_general(kc, q, (((1,), (1,)), ((), ())), preferred_element_type=F32)
            s_ref[pl.ds(c * kv_chunk, kv_chunk), :] = s
            cm = jnp.max(s, axis=0, keepdims=True)
            m = cm if m is None else jnp.maximum(m, cm)
        return m

    def softmax_pv(h, s_ref, m):
        l = jnp.zeros_like(m)
        acc = jnp.zeros((V_DIM, m.shape[1]), F32)
        row = pl.ds(pl.multiple_of(h * V_DIM, V_DIM), V_DIM)
        for c in range(n_chunks):
            p = jnp.exp2(s_ref[pl.ds(c * kv_chunk, kv_chunk), :] - m)
            l = l + jnp.sum(p, axis=0, keepdims=True)
            vc = vt_ref[0, row, pl.ds(c * kv_chunk, kv_chunk)]
            acc = acc + jnp.dot(vc, p.astype(BF16), preferred_element_type=F32)
        o_ref[0, row, :] = acc / l

    def pair(j, m_even):
        h = 2 * j
        m_odd = scores(h + 1, sb_ref)
        softmax_pv(h, sa_ref, m_even)
        m_next = scores(h + 2, sa_ref)
        softmax_pv(h + 1, sb_ref, m_odd)
        return m_next

    m6 = lax.fori_loop(0, N_HEADS // 2 - 1, pair, scores(0, sa_ref))
    m7 = scores(N_HEADS - 1, sb_ref)
    softmax_pv(N_HEADS - 2, sa_ref, m6)
    softmax_pv(N_HEADS - 1, sb_ref, m7)


def _attention(q, k, vt):
    bsz, _, n, _ = q.shape
    n_all = k.shape[2]
    n_chunks = KV_CHUNKS
    assert n_all % (n_chunks * LANES) == 0
    return pl.pallas_call(
        functools.partial(_attn_kernel, n_chunks),
        grid=(bsz, n // Q_TILE),
        in_specs=[pl.BlockSpec((1, N_HEADS, Q_TILE, HEAD_PAD), lambda b, t: (b, 0, t, 0)),
                  pl.BlockSpec((1, N_HEADS, n_all, HEAD_PAD), lambda b, t: (b, 0, 0, 0)),
                  pl.BlockSpec((1, ATTN_W, n_all), lambda b, t: (b, 0, 0))],
        out_specs=pl.BlockSpec((1, ATTN_W, Q_TILE), lambda b, t: (b, 0, t)),
        out_shape=jax.ShapeDtypeStruct((bsz, ATTN_W, n), F32),
        scratch_shapes=[pltpu.VMEM((n_all, Q_TILE), F32), pltpu.VMEM((n_all, Q_TILE), F32)],
        compiler_params=_cparams(("arbitrary", "arbitrary")),
        name="attention",
    )(q, k, vt)


def _ssm_kernel(bsz, u0_ref, u1_ref, wb_ref, wc_ref, are_ref, aim_ref, j_ref, y0_ref, y1_ref,
                bu_ref, st_ref):
    tc = SSM_CHUNK
    i = pl.program_id(0)

    @pl.when(i == 0)
    def _():
        st_ref[...] = jnp.zeros_like(st_ref)

    jrev = j_ref[...]
    for d in range(2):
        u_ref = u0_ref if d == 0 else u1_ref
        ub = u_ref[...].astype(BF16)
        if d == 1:
            ub = jnp.stack([jnp.dot(jrev, ub[b], preferred_element_type=F32).astype(BF16)
                            for b in range(bsz)])
        ub = ub.reshape(bsz * tc, SSM_W)
        for j in range(N_PAIRS):
            kb = (j // 4) * LANES
            r = jnp.dot(ub[:, kb:kb + LANES], wb_ref[d, j], preferred_element_type=F32)
            for b in range(bsz):
                row = (d * bsz + b) * SSM_PITCH
                bu_ref[2 * j, row:row + tc, :] = r[b * tc:(b + 1) * tc, 0:LANES]
                bu_ref[2 * j + 1, row:row + tc, :] = r[b * tc:(b + 1) * tc, LANES:2 * LANES]

    group = 4
    for j0 in range(0, N_PAIRS, group):
        are = [are_ref[j0 + g] for g in range(group)]
        aim = [aim_ref[j0 + g] for g in range(group)]
        init = tuple(st_ref[2 * (j0 + g) + c] for g in range(group) for c in range(2))

        def step(t, carry, j0=j0, are=are, aim=aim):
            out = []
            for g in range(group):
                sre, sim = carry[2 * g], carry[2 * g + 1]
                bre = bu_ref[2 * (j0 + g), pl.ds(t, N_SEQ, stride=SSM_PITCH), :]
                bim = bu_ref[2 * (j0 + g) + 1, pl.ds(t, N_SEQ, stride=SSM_PITCH), :]
                nre = are[g] * sre - aim[g] * sim + bre
                nim = are[g] * sim + aim[g] * sre + bim
                bu_ref[2 * (j0 + g), pl.ds(t, N_SEQ, stride=SSM_PITCH), :] = nre
                bu_ref[2 * (j0 + g) + 1, pl.ds(t, N_SEQ, stride=SSM_PITCH), :] = nim
                out += [nre, nim]
            return tuple(out)

        fin = lax.fori_loop(0, tc, step, init, unroll=2)
        for g in range(group):
            st_ref[2 * (j0 + g)] = fin[2 * g]
            st_ref[2 * (j0 + g) + 1] = fin[2 * g + 1]

    for d in range(2):
        y_ref = y0_ref if d == 0 else y1_ref
        for b in range(bsz):
            row = (d * bsz + b) * SSM_PITCH
            blocks = []
            for m in range(SSM_W // LANES):
                acc = None
                for j in range(4 * m, 4 * m + 4):
                    s = jnp.concatenate([bu_ref[2 * j, row:row + tc, :], bu_ref[2 * j + 1, row:row + tc, :]],
                                        axis=-1).astype(BF16)
                    t = jnp.dot(s, wc_ref[d, j], preferred_element_type=F32)
                    acc = t if acc is None else acc + t
                blocks.append(acc)
            y = jnp.concatenate(blocks, axis=-1)
            if d == 1:
                hi = y.astype(BF16)
                lo = (y - hi.astype(F32)).astype(BF16)
                y = (jnp.dot(jrev, hi, preferred_element_type=F32)
                     + jnp.dot(jrev, lo, preferred_element_type=F32))
            y_ref[b] = y


def _ssm(u_all, ssm):
    bsz, n_all, _ = u_all.shape
    assert N_SEQ == 2 * bsz
    nc = n_all // SSM_CHUNK
    ctx_chunks = 256 // SSM_CHUNK

    def rev_map(i):
        return jnp.where(i < ctx_chunks, ctx_chunks - 1 - i, nc - 1 - (i - ctx_chunks))

    blk = (bsz, SSM_CHUNK, SSM_W)
    return pl.pallas_call(
        functools.partial(_ssm_kernel, bsz),
        grid=(nc,),
        in_specs=[pl.BlockSpec(blk, lambda i: (0, i, 0)),
                  pl.BlockSpec(blk, lambda i: (0, rev_map(i), 0)),
                  _full(ssm["wb"].shape), _full(ssm["wc"].shape),
                  _full(ssm["a_re"].shape), _full(ssm["a_im"].shape), _full(ssm["jrev"].shape)],
        out_specs=[pl.BlockSpec(blk, lambda i: (0, i, 0)),
                   pl.BlockSpec(blk, lambda i: (0, rev_map(i), 0))],
        out_shape=[jax.ShapeDtypeStruct(u_all.shape, F32)] * 2,
        scratch_shapes=[pltpu.VMEM((2 * N_PAIRS, N_SEQ * SSM_PITCH, LANES), F32),
                        pltpu.VMEM((2 * N_PAIRS, N_SEQ, LANES), F32)],
        compiler_params=_cparams(("arbitrary",)),
        name="ssm",
    )(u_all, u_all, ssm["wb"], ssm["wc"], ssm["a_re"], ssm["a_im"], ssm["jrev"])


def _layer_norm(v, g, b):
    mu = jnp.mean(v, -1, keepdims=True)
    var = jnp.mean(jnp.square(v - mu), -1, keepdims=True)
    return (v - mu) * lax.rsqrt(var + EPS) * g + b


def _gelu_tanh(v):
    return 0.5 * v * (1.0 + jnp.tanh(math.sqrt(2.0 / math.pi) * (v + 0.044715 * (v * v * v))))


def _route(logits):
    lane = lax.broadcasted_iota(jnp.int32, logits.shape, 1).astype(F32)
    neg = jnp.float32(-jnp.inf)
    big = jnp.float32(1 << 20)
    gl = jnp.where((lane >= N_EXPERTS) & (lane < N_EXPERTS + N_EGROUPS), logits, neg)
    gmax = jnp.max(gl, -1, keepdims=True)
    g_w = 1.0 / jnp.sum(jnp.exp(gl - gmax), -1, keepdims=True)
    g_lane = jnp.min(jnp.where(gl == gmax, lane, big), -1, keepdims=True)
    lo = (g_lane - N_EXPERTS) * E_PER_GROUP
    el = jnp.where((lane >= lo) & (lane < lo + E_PER_GROUP), logits, neg)
    m1 = jnp.max(el, -1, keepdims=True)
    i1 = jnp.min(jnp.where(el == m1, lane, big), -1, keepdims=True)
    el2 = jnp.where(lane == i1, neg, el)
    m2 = jnp.max(el2, -1, keepdims=True)
    i2 = jnp.min(jnp.where(el2 == m2, lane, big), -1, keepdims=True)
    esum = jnp.sum(jnp.exp(el - m1), -1, keepdims=True)
    p1 = 1.0 / esum
    p2 = jnp.exp(m2 - m1) / esum
    w1 = p1 / (p1 + p2) * g_w
    w2 = p2 / (p1 + p2) * g_w
    onehot = jnp.where((lane == i1) | (lane == i2 + N_EXPERTS), 1.0, 0.0)
    info = jnp.where(lane == 0, i1, jnp.where(lane == 1, i2, jnp.where(lane == 2, w1, jnp.where(lane == 3, w2, 0.0))))
    return onehot, info


def _merge_kernel(at_ref, y0_ref, y1_ref, u_ref, x_ref, mod_ref, dsk_ref, wglu_ref, bglu_ref, gna_ref, gns_ref,
                  wo_ref, ln1g_ref, ln1b_ref, wrh_ref, wrl_ref, br_ref, x1_ref, hp_ref, oh_ref, info_ref, hs_ref):
    b = pl.program_id(0)
    mod = mod_ref[pl.ds(b, 1), :]
    gate_a = mod[:, 2 * D_MODEL:3 * D_MODEL]
    shift_f = mod[:, 3 * D_MODEL:4 * D_MODEL]
    scale_f = mod[:, 4 * D_MODEL:5 * D_MODEL]

    g = _gelu_tanh(y0_ref[0] + y1_ref[0] + dsk_ref[...] * u_ref[0])
    z = jnp.dot(g.astype(BF16), wglu_ref[...], preferred_element_type=F32) + bglu_ref[...]
    ssm = g * jax.nn.sigmoid(z)
    attn = at_ref[0].T
    an = _rms(attn, gna_ref[...]).astype(BF16)
    sn = _rms(ssm, gns_ref[...]).astype(BF16)
    mix = (jnp.dot(an, wo_ref[0:ATTN_W, :], preferred_element_type=F32)
           + jnp.dot(sn, wo_ref[ATTN_W:ATTN_W + SSM_W, :], preferred_element_type=F32))
    x1 = _layer_norm(ALPHA * x_ref[0] + gate_a * mix, ln1g_ref[...], ln1b_ref[...])
    x1_ref[0] = x1
    h = x1 * (1.0 + scale_f) + shift_f
    h_hi = h.astype(BF16)
    h_hif = h_hi.astype(F32)
    h_lo = (h - h_hif).astype(BF16)
    logits = (jnp.dot(h_hi, wrh_ref[...], preferred_element_type=F32)
              + jnp.dot(h_lo, wrh_ref[...], preferred_element_type=F32)
              + jnp.dot(h_hi, wrl_ref[...], preferred_element_type=F32)) + br_ref[...]
    onehot, info = _route(logits)
    oh_ref[...] = onehot.astype(BF16)
    info_ref[...] = info
    t = h.shape[0]
    for c in range(SLAB_ROWS):
        hs_ref[pl.ds(c, t, stride=SLAB_ROWS), :] = h[:, c * LANES:(c + 1) * LANES]
    hp_ref[...] = hs_ref[...].astype(BF16)


def _merge(attn_t, y0, y1, u_all, x, mod, prep, ctx_len):
    bsz, n, _ = x.shape
    off = ctx_len // ROW_TILE
    tok = lambda w: pl.BlockSpec((1, ROW_TILE, w), lambda b, t: (b, t, 0))
    tok_off = lambda w: pl.BlockSpec((1, ROW_TILE, w), lambda b, t: (b, t + off, 0))
    weights = [prep["d_skip"], prep["w_glu"], prep["b_glu"], prep["gn_attn"], prep["gn_ssm"], prep["w_o"],
               prep["ln1_g"], prep["ln1_b"], prep["w_router_hi"], prep["w_router_lo"], prep["b_router"]]
    nt = n // ROW_TILE
    flat = lambda rows: pl.BlockSpec((rows, LANES), lambda b, t: (b * nt + t, 0))
    return pl.pallas_call(
        _merge_kernel,
        grid=(bsz, nt),
        in_specs=[pl.BlockSpec((1, ATTN_W, ROW_TILE), lambda b, t: (b, 0, t)),
                  tok_off(SSM_W), tok_off(SSM_W), tok_off(SSM_W), tok(D_MODEL), _full(mod.shape)]
                 + [_full(w.shape) for w in weights],
        out_specs=[tok(D_MODEL), flat(ROW_TILE * SLAB_ROWS), flat(ROW_TILE), flat(ROW_TILE)],
        out_shape=[jax.ShapeDtypeStruct((bsz, n, D_MODEL), F32),
                   jax.ShapeDtypeStruct((bsz * n * SLAB_ROWS, LANES), BF16),
                   jax.ShapeDtypeStruct((bsz * n, LANES), BF16),
                   jax.ShapeDtypeStruct((bsz * n, LANES), F32)],
        scratch_shapes=[pltpu.VMEM((ROW_TILE * SLAB_ROWS, LANES), F32)],
        compiler_params=_cparams(("arbitrary", "arbitrary")),
        name="merge",
    )(attn_t, y0, y1, u_all, x, mod, *weights)


def _rank_kernel(oh_ref, tri_ref, rank_ref, cnt_ref, carry_ref):
    @pl.when(pl.program_id(0) == 0)
    def _():
        carry_ref[...] = jnp.zeros_like(carry_ref)

    oh = oh_ref[...]
    ohf = oh.astype(F32)
    before = jnp.dot(tri_ref[...], oh, preferred_element_type=F32) + carry_ref[0:1, :]
    lane = lax.broadcasted_iota(jnp.int32, ohf.shape, 1)
    hit = ohf * before
    r1 = jnp.sum(jnp.where(lane < N_EXPERTS, hit, 0.0), -1, keepdims=True)
    r2 = jnp.sum(jnp.where(lane >= N_EXPERTS, hit, 0.0), -1, keepdims=True)
    rank_ref[...] = jnp.where(lane == 0, r1, jnp.where(lane == 1, r2, 0.0))
    total = carry_ref[...] + jnp.sum(ohf, 0, keepdims=True)
    carry_ref[...] = total
    cnt_ref[...] = total


def _rank(onehot):
    rows = onehot.shape[0]
    tile = min(RANK_TILE, rows)
    tri = jnp.tril(jnp.ones((tile, tile), F32), -1).astype(BF16)
    return pl.pallas_call(
        _rank_kernel,
        grid=(rows // tile,),
        in_specs=[pl.BlockSpec((tile, LANES), lambda i: (i, 0)), _full(tri.shape)],
        out_specs=[pl.BlockSpec((tile, LANES), lambda i: (i, 0)), _full((8, LANES))],
        out_shape=[jax.ShapeDtypeStruct((rows, LANES), F32), jax.ShapeDtypeStruct((8, LANES), F32)],
        scratch_shapes=[pltpu.VMEM((8, LANES), F32)],
        compiler_params=_cparams(("arbitrary",)),
        name="rank",
    )(onehot, tri)


def _expert_kernel(te_ref, tv_ref, src_ref, dst_ref, hp_hbm, ws_ref, wg_ref, wu_ref, wd_ref, y_hbm,
                   hres_ref, xbuf_ref, yslab_ref, pend_ref, load_sem, out_sem):
    i = pl.program_id(0)
    last = pl.num_programs(0) - 1
    slot = i % 2
    tm = EXPERT_TILE

    def slab_done(s):
        return pltpu.make_async_copy(yslab_ref.at[s], y_hbm.at[pl.ds(0, tm * SLAB_ROWS), :], out_sem.at[s])

    @pl.when(i == 0)
    def _():
        pend_ref[0] = 0
        pend_ref[1] = 0
        cp = pltpu.make_async_copy(hp_hbm, hres_ref, load_sem)
        cp.start()
        cp.wait()

    @pl.when(tv_ref[i] == 1)
    def _():
        @pl.when(pend_ref[slot] == 1)
        def _():
            slab_done(slot).wait()
            pend_ref[slot] = 0

        for r in range(tm):
            tok = src_ref[0, 0, r]
            pair = hres_ref[pl.ds(pl.multiple_of((tok >> 1) * (2 * SLAB_ROWS), 2 * SLAB_ROWS), 2 * SLAB_ROWS), :]
            pair = pair.astype(F32)
            xbuf_ref[pl.ds(r, SLAB_ROWS, stride=XBUF_PITCH), :] = jnp.where(
                (tok & 1) == 1, pair[SLAB_ROWS:2 * SLAB_ROWS], pair[0:SLAB_ROWS])
        x = jnp.concatenate([xbuf_ref[c * XBUF_PITCH:c * XBUF_PITCH + tm, :] for c in range(SLAB_ROWS)],
                            axis=-1).astype(BF16)
        a = jnp.dot(x, wg_ref[0].astype(BF16), preferred_element_type=F32)
        u = jnp.dot(x, wu_ref[0].astype(BF16), preferred_element_type=F32)
        act = (a * jax.nn.sigmoid(a) * u * ws_ref[...]).astype(BF16)
        y = jnp.dot(act, wd_ref[0].astype(BF16), preferred_element_type=F32)
        for c in range(SLAB_ROWS):
            yslab_ref[slot, pl.ds(c, tm, stride=SLAB_ROWS), :] = y[:, c * LANES:(c + 1) * LANES]
        for r in range(tm):
            d = dst_ref[0, 0, r]
            pltpu.make_async_copy(yslab_ref.at[slot, pl.ds(r * SLAB_ROWS, SLAB_ROWS), :],
                                  y_hbm.at[pl.ds(pl.multiple_of(d * SLAB_ROWS, SLAB_ROWS), SLAB_ROWS), :],
                                  out_sem.at[slot]).start()
        pend_ref[slot] = 1

    @pl.when(i == last)
    def _():
        for s in range(2):
            @pl.when(pend_ref[s] == 1)
            def _():
                slab_done(s).wait()
                pend_ref[s] = 0


def _experts(tile_e, tile_v, src, dst, hp, w_sorted, wg, wu, wd, n_out_rows):
    n_tiles = tile_e.shape[0]
    tm = EXPERT_TILE
    wspec = lambda shape: pl.BlockSpec((1,) + shape, lambda i, te, tv: (te[i], 0, 0))
    idx_spec = pl.BlockSpec((1, 1, tm), lambda i, te, tv: (i, 0, 0), memory_space=pltpu.SMEM)
    grid_spec = pltpu.PrefetchScalarGridSpec(
        num_scalar_prefetch=2,
        grid=(n_tiles,),
        in_specs=[idx_spec, idx_spec,
                  pl.BlockSpec(memory_space=pl.ANY),
                  pl.BlockSpec((tm, 1), lambda i, te, tv: (i, 0)),
                  wspec((D_MODEL, D_FF)), wspec((D_MODEL, D_FF)), wspec((D_FF, D_MODEL))],
        out_specs=pl.BlockSpec(memory_space=pl.ANY),
        scratch_shapes=[pltpu.VMEM(hp.shape, BF16),
                        pltpu.VMEM((SLAB_ROWS * XBUF_PITCH, LANES), F32),
                        pltpu.VMEM((2, tm * SLAB_ROWS, LANES), F32),
                        pltpu.SMEM((2,), jnp.int32),
                        pltpu.SemaphoreType.DMA(()),
                        pltpu.SemaphoreType.DMA((2,))])
    return pl.pallas_call(
        _expert_kernel,
        grid_spec=grid_spec,
        out_shape=jax.ShapeDtypeStruct((n_out_rows * SLAB_ROWS, LANES), F32),
        compiler_params=pltpu.CompilerParams(dimension_semantics=("arbitrary",),
                                             vmem_limit_bytes=EXPERT_VMEM_LIMIT),
        name="experts",
    )(tile_e, tile_v, src.reshape(n_tiles, 1, tm), dst.reshape(n_tiles, 1, tm), hp, w_sorted, wg, wu, wd)


def _combine_kernel(y0_ref, y1_ref, x1_ref, mod_ref, ln2g_ref, ln2b_ref, o_ref):
    b = pl.program_id(0)
    gate_f = mod_ref[pl.ds(b, 1), 5 * D_MODEL:6 * D_MODEL]
    t = x1_ref.shape[1]
    ffn = jnp.concatenate([y0_ref[pl.ds(c, t, stride=SLAB_ROWS), :] + y1_ref[pl.ds(c, t, stride=SLAB_ROWS), :]
                           for c in range(SLAB_ROWS)], axis=-1)
    o_ref[0] = _layer_norm(ALPHA * x1_ref[0] + gate_f * ffn, ln2g_ref[...], ln2b_ref[...])


def _combine(y_slots, x1, mod, ln2_g, ln2_b):
    bsz, n, _ = x1.shape
    nt = n // ROW_TILE
    slab = lambda k: pl.BlockSpec((ROW_TILE * SLAB_ROWS, LANES), lambda b, t: (k * bsz * nt + b * nt + t, 0))
    return pl.pallas_call(
        _combine_kernel,
        grid=(bsz, nt),
        in_specs=[slab(0), slab(1), pl.BlockSpec((1, ROW_TILE, D_MODEL), lambda b, t: (b, t, 0)),
                  _full(mod.shape), _full(ln2_g.shape), _full(ln2_b.shape)],
        out_specs=pl.BlockSpec((1, ROW_TILE, D_MODEL), lambda b, t: (b, t, 0)),
        out_shape=jax.ShapeDtypeStruct((bsz, n, D_MODEL), F32),
        compiler_params=_cparams(("arbitrary", "arbitrary")),
        name="combine",
    )(y_slots, y_slots, x1, mod, ln2_g, ln2_b)


def _dispatch_tables(info, rank, counts, n_tok):
    tm = EXPERT_TILE
    e1 = info[:, 0].astype(jnp.int32)
    e2 = info[:, 1].astype(jnp.int32)
    w = jnp.concatenate([info[:, 2], info[:, 3]])
    r1 = rank[:, 0].astype(jnp.int32)
    r2 = rank[:, 1].astype(jnp.int32)
    c1 = counts[0, 0:N_EXPERTS].astype(jnp.int32)
    c2 = counts[0, N_EXPERTS:2 * N_EXPERTS].astype(jnp.int32)
    tiles = (c1 + c2 + tm - 1) // tm
    ends = jnp.cumsum(tiles)
    base = (ends - tiles) * tm
    pos = jnp.concatenate([base[e1] + r1, base[e2] + c1[e2] + r2])
    n_tiles = 2 * n_tok // tm + N_EXPERTS
    n_rows = n_tiles * tm
    tok = jnp.arange(n_tok, dtype=jnp.int32)
    src = jnp.zeros((n_rows,), jnp.int32).at[pos].set(jnp.concatenate([tok, tok]), unique_indices=True)
    dst = (2 * n_tok + jnp.arange(n_rows, dtype=jnp.int32)).at[pos].set(
        jnp.concatenate([tok, n_tok + tok]), unique_indices=True)
    w_sorted = jnp.zeros((n_rows,), F32).at[pos].set(w, unique_indices=True).reshape(n_rows, 1)
    tile_id = jnp.arange(n_tiles, dtype=jnp.int32)
    tile_v = (tile_id < ends[-1]).astype(jnp.int32)
    tile_e = jnp.sum((tile_id[:, None] >= ends[None, :]).astype(jnp.int32), axis=1)
    tile_e = jnp.minimum(tile_e, N_EXPERTS - 1)
    last_e = tile_e[jnp.maximum(ends[-1] - 1, 0)]
    tile_e = jnp.where(tile_v == 1, tile_e, last_e)
    return tile_e, tile_v, src, dst, w_sorted, 2 * n_tok + n_rows


def _rope_tables(n):
    rows = n // GRID_W
    row = jnp.repeat(jnp.arange(rows), GRID_W).astype(F32)
    col = jnp.tile(jnp.arange(GRID_W), rows).astype(F32)
    axis_dim = QK_ROPE // 2
    inv_freq = ROPE_BASE ** (-jnp.arange(0, axis_dim, 2, dtype=F32) / axis_dim)
    ang = jnp.concatenate([row[:, None] * inv_freq, col[:, None] * inv_freq], -1)
    ang = jnp.concatenate([ang, ang], -1)
    pad_l = jnp.ones((n, QK_NOPE), F32)
    pad_r = jnp.ones((n, HEAD_PAD - QK_NOPE - QK_ROPE), F32)
    cos_t = jnp.concatenate([pad_l, jnp.cos(ang), pad_r], -1)
    sin_t = jnp.concatenate([0 * pad_l, jnp.sin(ang), 0 * pad_r], -1)
    return cos_t, sin_t


def _rot_half_cols(w):
    w1, w2 = w[..., :QK_ROPE // 2], w[..., QK_ROPE // 2:]
    return jnp.concatenate([-w2, w1], -1)


def _prep_project(w_in, q_norm_g, kv_norm_g, w_uq, w_ukv):
    zeros = lambda r, c: jnp.zeros((r, c), F32)
    w_in_p = jnp.concatenate([w_in[:, :MLA_IN], zeros(D_MODEL, 512 - MLA_IN), w_in[:, MLA_IN:]], -1)
    qk_scale = SM_SCALE * math.log2(math.e)
    wq = (w_uq * qk_scale).reshape(Q_RANK, N_HEADS, QK_NOPE + QK_ROPE)
    pad = jnp.zeros((Q_RANK, N_HEADS, HEAD_PAD - QK_NOPE - QK_ROPE), F32)
    wq_pad = jnp.concatenate([wq, pad], -1).reshape(Q_RANK, N_HEADS * HEAD_PAD)
    wq_rot = jnp.concatenate([0 * wq[..., :QK_NOPE], _rot_half_cols(wq[..., QK_NOPE:]), pad], -1)
    wq_rot = wq_rot.reshape(Q_RANK, N_HEADS * HEAD_PAD)
    wkv = w_ukv.reshape(KV_RANK, N_HEADS, QK_NOPE + V_DIM)
    wk = jnp.concatenate([wkv[..., :QK_NOPE], jnp.zeros((KV_RANK, N_HEADS, HEAD_PAD - QK_NOPE), F32)], -1)
    wk = wk.reshape(KV_RANK, N_HEADS * HEAD_PAD)
    wvt = wkv[..., QK_NOPE:].reshape(KV_RANK, ATTN_W).T
    eye = jnp.eye(QK_ROPE, dtype=F32)
    place = lambda m: jnp.zeros((LANES, HEAD_PAD), F32).at[:QK_ROPE, QK_NOPE:QK_NOPE + QK_ROPE].set(m)
    return dict(w_in=w_in_p.astype(BF16), q_g=q_norm_g.reshape(1, -1), kv_g=kv_norm_g.reshape(1, -1),
                wq=wq_pad.astype(BF16), wq_rot=wq_rot.astype(BF16), wk=wk.astype(BF16),
                wvt=wvt.astype(BF16), e=place(eye).astype(BF16), e_rot=place(_rot_half_cols(eye)).astype(BF16))


def _discretise(a_re, a_im, log_dt, b_re, b_im):
    dt = jnp.exp(log_dt)[:, None]
    mag = jnp.exp(a_re * dt)
    abar_re, abar_im = mag * jnp.cos(a_im * dt), mag * jnp.sin(a_im * dt)
    den = jnp.square(a_re) + jnp.square(a_im)
    num_re = abar_re - 1.0
    coef_re = (num_re * a_re + abar_im * a_im) / den
    coef_im = (abar_im * a_re - num_re * a_im) / den
    bbar_re = coef_re[..., None] * b_re - coef_im[..., None] * b_im
    bbar_im = coef_re[..., None] * b_im + coef_im[..., None] * b_re
    return abar_re, abar_im, bbar_re, bbar_im


def _prep_ssm(a_re, a_im, log_dt, b_re, b_im, c_re, c_im, bsz):
    wbs, wcs, ares, aims = [], [], [], []
    eye2 = jnp.eye(2, dtype=F32)
    for d in range(2):
        abar_re, abar_im, bbar_re, bbar_im = _discretise(a_re[d], a_im[d], log_dt[d], b_re[d], b_im[d])

        def in_block(bb):
            t = bb.reshape(N_PAIRS, 2, SSM_STATE, SSM_GROUP)
            return jnp.einsum('jgph,gk->jghkp', t, eye2).reshape(N_PAIRS, 2 * SSM_GROUP, 2 * SSM_STATE)

        blk = jnp.concatenate([in_block(bbar_re), in_block(bbar_im)], -1)
        wb = jnp.zeros((N_PAIRS, LANES, 2 * LANES), F32)
        for j in range(N_PAIRS):
            r0 = 32 * (j % 4)
            wb = wb.at[j, r0:r0 + 32, :].set(blk[j])
        wbs.append(wb)

        def out_block(cc):
            t = cc.reshape(N_PAIRS, 2, SSM_GROUP, SSM_STATE)
            return jnp.einsum('jghp,gk->jgpkh', t, eye2).reshape(N_PAIRS, 2 * SSM_STATE, 2 * SSM_GROUP)

        oblk = jnp.concatenate([out_block(c_re[d]), -out_block(c_im[d])], 1)
        wc = jnp.zeros((N_PAIRS, 2 * LANES, LANES), F32)
        for j in range(N_PAIRS):
            c0 = 32 * (j % 4)
            wc = wc.at[j, :, c0:c0 + 32].set(oblk[j])
        wcs.append(wc)
        ares.append(jnp.broadcast_to(abar_re.reshape(N_PAIRS, 1, LANES), (N_PAIRS, bsz, LANES)))
        aims.append(jnp.broadcast_to(abar_im.reshape(N_PAIRS, 1, LANES), (N_PAIRS, bsz, LANES)))
    jrev = jnp.eye(SSM_CHUNK, dtype=F32)[::-1]
    return dict(wb=jnp.stack(wbs).astype(BF16), wc=jnp.stack(wcs).astype(BF16),
                a_re=jnp.concatenate(ares, 1), a_im=jnp.concatenate(aims, 1), jrev=jrev.astype(BF16))


def kernel(x, c, ctx, c_ctx, w_ada, b_ada, w_in, q_norm_g, kv_norm_g, w_uq, w_ukv, ssm_a_re, ssm_a_im,
           ssm_log_dt, ssm_b_re, ssm_b_im, ssm_c_re, ssm_c_im, ssm_d, w_glu, b_glu, gn_attn_g, gn_ssm_g, w_o,
           ln1_g, ln1_b, w_router_group, b_router_group, w_router_expert, b_router_expert, w_exp_gate,
           w_exp_up, w_exp_down, ln2_g, ln2_b):
    bsz, n, _ = x.shape
    ctx_len = ctx.shape[1]
    n_all = ctx_len + n
    l = 0
    row = lambda v: v.reshape(1, -1)

    cvec = jnp.concatenate([c, c_ctx[None, :], jnp.zeros((8 - bsz - 1, D_MODEL), F32)], 0)
    mod = _adaln(cvec, w_ada[l], b_ada[l])

    prep = _prep_project(w_in[l], q_norm_g[l], kv_norm_g[l], w_uq[l], w_ukv[l])
    cos_t, sin_t = _rope_tables(n)
    ones_t = jnp.ones((ctx_len, HEAD_PAD), F32)
    q, k, vt, u_all = _project(x, mod, cos_t, sin_t, prep, n_all, ctx_len)
    k, vt, u_all = _project(ctx, mod, ones_t, 0 * ones_t, prep, n_all, 0, prev=(k, vt, u_all))

    attn_t = _attention(q, k, vt)

    ssm = _prep_ssm(ssm_a_re[l], ssm_a_im[l], ssm_log_dt[l], ssm_b_re[l], ssm_b_im[l],
                    ssm_c_re[l], ssm_c_im[l], bsz)
    y0, y1 = _ssm(u_all, ssm)

    w_router = jnp.concatenate([w_router_expert[l], w_router_group[l],
                                jnp.zeros((D_MODEL, LANES - N_EXPERTS - N_EGROUPS), F32)], -1)
    b_router = jnp.concatenate([b_router_expert[l], b_router_group[l],
                                jnp.zeros((LANES - N_EXPERTS - N_EGROUPS,), F32)])
    w_router_hi = w_router.astype(BF16)
    mprep = dict(d_skip=row(ssm_d[l]), w_glu=w_glu[l].astype(BF16), b_glu=row(b_glu[l]),
                 gn_attn=row(gn_attn_g[l]), gn_ssm=row(gn_ssm_g[l]), w_o=w_o[l].astype(BF16),
                 ln1_g=row(ln1_g[l]), ln1_b=row(ln1_b[l]), w_router_hi=w_router_hi,
                 w_router_lo=(w_router - w_router_hi.astype(F32)).astype(BF16), b_router=row(b_router))
    x1, hp, onehot, info = _merge(attn_t, y0, y1, u_all, x, mod, mprep, ctx_len)

    rank, counts = _rank(onehot)
    tile_e, tile_v, src, dst, w_sorted, n_out_rows = _dispatch_tables(info, rank, counts, bsz * n)
    y_slots = _experts(tile_e, tile_v, src, dst, hp, w_sorted, w_exp_gate[l], w_exp_up[l], w_exp_down[l],
                       n_out_rows)
    return _combine(y_slots, x1, mod, row(ln2_g[l]), row(ln2_b[l]))
```

```python
import functools
import math

import jax
import jax.numpy as jnp
from jax import lax
from jax.experimental import pallas as pl
from jax.experimental.pallas import tpu as pltpu

F32 = jnp.float32
BF16 = jnp.bfloat16

D_MODEL = 1024
GRID_W = 64
N_HEADS = 8
QK_NOPE = 64
QK_ROPE = 32
V_DIM = 64
Q_RANK = 256
KV_RANK = 128
ATTN_W = N_HEADS * V_DIM
MLA_IN = Q_RANK + KV_RANK + QK_ROPE
ROPE_BASE = 10000.0
SM_SCALE = (QK_NOPE + QK_ROPE) ** -0.5
SSM_W = D_MODEL - ATTN_W
SSM_GROUP = 16
N_GROUPS = SSM_W // SSM_GROUP
SSM_STATE = 64
N_EGROUPS = 4
E_PER_GROUP = 8
N_EXPERTS = N_EGROUPS * E_PER_GROUP
D_FF = 256
DEPTH = 1
ALPHA = (2 * DEPTH) ** 0.25
EPS = 1e-6

LANES = 128
HEAD_PAD = LANES
VMEM_LIMIT = 48 * 1024 * 1024

ROW_TILE = 256
Q_TILE = 256
KV_CHUNKS = 2
SSM_CHUNK = 128
SSM_PITCH = SSM_CHUNK + 8
N_SEQ = 8
N_PAIRS = N_GROUPS // 2
RANK_TILE = 512
EXPERT_TILE = 256
INVERT_CHUNK = 2048
SLAB_ROWS = D_MODEL // LANES
XBUF_PITCH = EXPERT_TILE + 8
EXPERT_VMEM_LIMIT = 56 * 1024 * 1024


def _cparams(sem):
    return pltpu.CompilerParams(dimension_semantics=sem, vmem_limit_bytes=VMEM_LIMIT)


def _full(shape):
    return pl.BlockSpec(shape, lambda *_: (0,) * len(shape))


def _adaln_kernel(c_ref, w_ref, b_ref, o_ref):
    c = c_ref[...]
    s = c * jax.nn.sigmoid(c)
    o_ref[...] = jnp.dot(s, w_ref[...], preferred_element_type=F32,
                         precision=lax.Precision.HIGHEST) + b_ref[...]


def _adaln(cvec, w_ada, b_ada):
    n_out = w_ada.shape[1]
    tn = 1536
    return pl.pallas_call(
        _adaln_kernel,
        grid=(n_out // tn,),
        in_specs=[_full(cvec.shape),
                  pl.BlockSpec((D_MODEL, tn), lambda j: (0, j)),
                  pl.BlockSpec((1, tn), lambda j: (0, j))],
        out_specs=pl.BlockSpec((cvec.shape[0], tn), lambda j: (0, j)),
        out_shape=jax.ShapeDtypeStruct((cvec.shape[0], n_out), F32),
        compiler_params=_cparams(("arbitrary",)),
        name="adaln",
    )(cvec, w_ada, b_ada.reshape(1, n_out))


def _rms(v, g):
    return v * lax.rsqrt(jnp.mean(jnp.square(v), -1, keepdims=True) + EPS) * g


def _project_kernel(with_q, mod_row_fn, x_ref, mod_ref, cos_ref, sin_ref, w_in_ref, qg_ref, kvg_ref,
                    wq_ref, wqr_ref, wk_ref, wvt_ref, e_ref, er_ref, *rest):
    if with_q:
        q_ref, k_ref, vt_ref, u_ref = rest
    else:
        k_ref, vt_ref, u_ref = rest[-3:]
    row = mod_row_fn(pl.program_id(0))
    mod = mod_ref[pl.ds(row, 1), :]
    shift = mod[:, 0:D_MODEL]
    scale = mod[:, D_MODEL:2 * D_MODEL]
    xm = (x_ref[0] * (1.0 + scale) + shift).astype(BF16)
    p = jnp.dot(xm, w_in_ref[...], preferred_element_type=F32)
    u_ref[0] = p[:, 512:1024]
    cos = cos_ref[...]
    sin = sin_ref[...]

    ckvn = _rms(p[:, Q_RANK:Q_RANK + KV_RANK], kvg_ref[...]).astype(BF16)
    kn = jnp.dot(ckvn, wk_ref[...], preferred_element_type=F32)
    kr = p[:, 384:512].astype(BF16)
    krope = (jnp.dot(kr, e_ref[...], preferred_element_type=F32) * cos
             + jnp.dot(kr, er_ref[...], preferred_element_type=F32) * sin)
    for h in range(N_HEADS):
        k_ref[0, h] = (kn[:, h * HEAD_PAD:(h + 1) * HEAD_PAD] + krope).astype(BF16)
    vt_ref[0] = lax.dot_general(wvt_ref[...], ckvn, (((1,), (1,)), ((), ())),
                                preferred_element_type=F32).astype(BF16)

    if with_q:
        qn = _rms(p[:, 0:Q_RANK], qg_ref[...]).astype(BF16)
        qa = jnp.dot(qn, wq_ref[...], preferred_element_type=F32)
        qb = jnp.dot(qn, wqr_ref[...], preferred_element_type=F32)
        for h in range(N_HEADS):
            sl = slice(h * HEAD_PAD, (h + 1) * HEAD_PAD)
            q_ref[0, h] = (qa[:, sl] * cos + qb[:, sl] * sin).astype(BF16)


def _project(x, mod, cos_t, sin_t, prep, n_all, row_off, prev=None):
    bsz, n, _ = x.shape
    with_q = prev is None
    nt = n // ROW_TILE
    off = row_off // ROW_TILE
    mod_row_fn = (lambda b: b) if with_q else (lambda b: bsz)
    w_specs = [_full(prep["w_in"].shape), _full(prep["q_g"].shape), _full(prep["kv_g"].shape),
               _full(prep["wq"].shape), _full(prep["wq_rot"].shape), _full(prep["wk"].shape),
               _full(prep["wvt"].shape), _full(prep["e"].shape), _full(prep["e_rot"].shape)]
    in_specs = [pl.BlockSpec((1, ROW_TILE, D_MODEL), lambda b, t: (b, t, 0)),
                _full(mod.shape),
                pl.BlockSpec((ROW_TILE, HEAD_PAD), lambda b, t: (t, 0)),
                pl.BlockSpec((ROW_TILE, HEAD_PAD), lambda b, t: (t, 0))] + w_specs
    k_spec = pl.BlockSpec((1, N_HEADS, ROW_TILE, HEAD_PAD), lambda b, t: (b, 0, t + off, 0))
    vt_spec = pl.BlockSpec((1, ATTN_W, ROW_TILE), lambda b, t: (b, 0, t + off))
    u_spec = pl.BlockSpec((1, ROW_TILE, SSM_W), lambda b, t: (b, t + off, 0))
    k_shape = jax.ShapeDtypeStruct((bsz, N_HEADS, n_all, HEAD_PAD), BF16)
    vt_shape = jax.ShapeDtypeStruct((bsz, ATTN_W, n_all), BF16)
    u_shape = jax.ShapeDtypeStruct((bsz, n_all, SSM_W), F32)
    args = [x, mod, cos_t, sin_t, prep["w_in"], prep["q_g"], prep["kv_g"], prep["wq"], prep["wq_rot"],
            prep["wk"], prep["wvt"], prep["e"], prep["e_rot"]]
    if with_q:
        out_specs = [pl.BlockSpec((1, N_HEADS, ROW_TILE, HEAD_PAD), lambda b, t: (b, 0, t, 0)),
                     k_spec, vt_spec, u_spec]
        out_shape = [jax.ShapeDtypeStruct((bsz, N_HEADS, n, HEAD_PAD), BF16), k_shape, vt_shape, u_shape]
        aliases = {}
    else:
        k_prev, vt_prev, u_prev = prev
        in_specs += [pl.BlockSpec(memory_space=pl.ANY)] * 3
        args += [k_prev, vt_prev, u_prev]
        out_specs = [k_spec, vt_spec, u_spec]
        out_shape = [k_shape, vt_shape, u_shape]
        aliases = {13: 0, 14: 1, 15: 2}
    return pl.pallas_call(
        functools.partial(_project_kernel, with_q, mod_row_fn), grid=(bsz, nt), in_specs=in_specs, out_specs=out_specs, out_shape=out_shape,
        input_output_aliases=aliases,
        compiler_params=_cparams(("arbitrary", "arbitrary")),
        name="project_lat" if with_q else "project_ctx",
    )(*args)


def _attn_kernel(n_chunks, q_ref, k_ref, vt_ref, o_ref, sa_ref, sb_ref):
    kv_chunk = k_ref.shape[2] // n_chunks

    def scores(h, s_ref):
        q = q_ref[0, h]
        m = None
        for c in range(n_chunks):
            kc = k_ref[0, h, pl.ds(c * kv_chunk, kv_chunk), :]
            s = lax.dot_general(kc, q, (((1,), (1,)), ((), ())), preferred_element_type=F32)
            s_ref[pl.ds(c * kv_chunk, kv_chunk), :] = s
            cm = jnp.max(s, axis=0, keepdims=True)
            m = cm if m is None else jnp.maximum(m, cm)
        return m

    def softmax_pv(h, s_ref, m):
        l = jnp.zeros_like(m)
        acc = jnp.zeros((V_DIM, m.shape[1]), F32)
        row = pl.ds(pl.multiple_of(h * V_DIM, V_DIM), V_DIM)
        for c in range(n_chunks):
            p = jnp.exp2(s_ref[pl.ds(c * kv_chunk, kv_chunk), :] - m)
            l = l + jnp.sum(p, axis=0, keepdims=True)
            vc = vt_ref[0, row, pl.ds(c * kv_chunk, kv_chunk)]
            acc = acc + jnp.dot(vc, p.astype(BF16), preferred_element_type=F32)
        o_ref[0, row, :] = acc / l

    def pair(j, m_even):
        h = 2 * j
        m_odd = scores(h + 1, sb_ref)
        softmax_pv(h, sa_ref, m_even)
        m_next = scores(h + 2, sa_ref)
        softmax_pv(h + 1, sb_ref, m_odd)
        return m_next

    m6 = lax.fori_loop(0, N_HEADS // 2 - 1, pair, scores(0, sa_ref))
    m7 = scores(N_HEADS - 1, sb_ref)
    softmax_pv(N_HEADS - 2, sa_ref, m6)
    softmax_pv(N_HEADS - 1, sb_ref, m7)


def _attention(q, k, vt):
    bsz, _, n, _ = q.shape
    n_all = k.shape[2]
    n_chunks = KV_CHUNKS
    assert n_all % (n_chunks * LANES) == 0
    return pl.pallas_call(
        functools.partial(_attn_kernel, n_chunks),
        grid=(bsz, n // Q_TILE),
        in_specs=[pl.BlockSpec((1, N_HEADS, Q_TILE, HEAD_PAD), lambda b, t: (b, 0, t, 0)),
                  pl.BlockSpec((1, N_HEADS, n_all, HEAD_PAD), lambda b, t: (b, 0, 0, 0)),
                  pl.BlockSpec((1, ATTN_W, n_all), lambda b, t: (b, 0, 0))],
        out_specs=pl.BlockSpec((1, ATTN_W, Q_TILE), lambda b, t: (b, 0, t)),
        out_shape=jax.ShapeDtypeStruct((bsz, ATTN_W, n), F32),
        scratch_shapes=[pltpu.VMEM((n_all, Q_TILE), F32), pltpu.VMEM((n_all, Q_TILE), F32)],
        compiler_params=_cparams(("arbitrary", "arbitrary")),
        name="attention",
    )(q, k, vt)


def _ssm_kernel(bsz, u0_ref, u1_ref, wb_ref, wc_ref, are_ref, aim_ref, j_ref, y0_ref, y1_ref,
                bu_ref, st_ref):
    tc = SSM_CHUNK
    i = pl.program_id(0)

    @pl.when(i == 0)
    def _():
        st_ref[...] = jnp.zeros_like(st_ref)

    jrev = j_ref[...]
    for d in range(2):
        u_ref = u0_ref if d == 0 else u1_ref
        ub = u_ref[...].astype(BF16)
        if d == 1:
            ub = jnp.stack([jnp.dot(jrev, ub[b], preferred_element_type=F32).astype(BF16)
                            for b in range(bsz)])
        ub = ub.reshape(bsz * tc, SSM_W)
        for j in range(N_PAIRS):
            kb = (j // 4) * LANES
            r = jnp.dot(ub[:, kb:kb + LANES], wb_ref[d, j], preferred_element_type=F32)
            for b in range(bsz):
                row = (d * bsz + b) * SSM_PITCH
                bu_ref[2 * j, row:row + tc, :] = r[b * tc:(b + 1) * tc, 0:LANES]
                bu_ref[2 * j + 1, row:row + tc, :] = r[b * tc:(b + 1) * tc, LANES:2 * LANES]

    group = 4
    for j0 in range(0, N_PAIRS, group):
        are = [are_ref[j0 + g] for g in range(group)]
        aim = [aim_ref[j0 + g] for g in range(group)]
        init = tuple(st_ref[2 * (j0 + g) + c] for g in range(group) for c in range(2))

        def step(t, carry, j0=j0, are=are, aim=aim):
            out = []
            for g in range(group):
                sre, sim = carry[2 * g], carry[2 * g + 1]
                bre = bu_ref[2 * (j0 + g), pl.ds(t, N_SEQ, stride=SSM_PITCH), :]
                bim = bu_ref[2 * (j0 + g) + 1, pl.ds(t, N_SEQ, stride=SSM_PITCH), :]
                nre = are[g] * sre - aim[g] * sim + bre
                nim = are[g] * sim + aim[g] * sre + bim
                bu_ref[2 * (j0 + g), pl.ds(t, N_SEQ, stride=SSM_PITCH), :] = nre
                bu_ref[2 * (j0 + g) + 1, pl.ds(t, N_SEQ, stride=SSM_PITCH), :] = nim
                out += [nre, nim]
            return tuple(out)

        fin = lax.fori_loop(0, tc, step, init, unroll=2)
        for g in range(group):
            st_ref[2 * (j0 + g)] = fin[2 * g]
            st_ref[2 * (j0 + g) + 1] = fin[2 * g + 1]

    for d in range(2):
        y_ref = y0_ref if d == 0 else y1_ref
        for b in range(bsz):
            row = (d * bsz + b) * SSM_PITCH
            blocks = []
            for m in range(SSM_W // LANES):
                acc = None
                for j in range(4 * m, 4 * m + 4):
                    s = jnp.concatenate([bu_ref[2 * j, row:row + tc, :], bu_ref[2 * j + 1, row:row + tc, :]],
                                        axis=-1).astype(BF16)
                    t = jnp.dot(s, wc_ref[d, j], preferred_element_type=F32)
                    acc = t if acc is None else acc + t
                blocks.append(acc)
            y = jnp.concatenate(blocks, axis=-1)
            if d == 1:
                hi = y.astype(BF16)
                lo = (y - hi.astype(F32)).astype(BF16)
                y = (jnp.dot(jrev, hi, preferred_element_type=F32)
                     + jnp.dot(jrev, lo, preferred_element_type=F32))
            y_ref[b] = y


def _ssm(u_all, ssm):
    bsz, n_all, _ = u_all.shape
    assert N_SEQ == 2 * bsz
    nc = n_all // SSM_CHUNK
    ctx_chunks = 256 // SSM_CHUNK

    def rev_map(i):
        return jnp.where(i < ctx_chunks, ctx_chunks - 1 - i, nc - 1 - (i - ctx_chunks))

    blk = (bsz, SSM_CHUNK, SSM_W)
    return pl.pallas_call(
        functools.partial(_ssm_kernel, bsz),
        grid=(nc,),
        in_specs=[pl.BlockSpec(blk, lambda i: (0, i, 0)),
                  pl.BlockSpec(blk, lambda i: (0, rev_map(i), 0)),
                  _full(ssm["wb"].shape), _full(ssm["wc"].shape),
                  _full(ssm["a_re"].shape), _full(ssm["a_im"].shape), _full(ssm["jrev"].shape)],
        out_specs=[pl.BlockSpec(blk, lambda i: (0, i, 0)),
                   pl.BlockSpec(blk, lambda i: (0, rev_map(i), 0))],
        out_shape=[jax.ShapeDtypeStruct(u_all.shape, F32)] * 2,
        scratch_shapes=[pltpu.VMEM((2 * N_PAIRS, N_SEQ * SSM_PITCH, LANES), F32),
                        pltpu.VMEM((2 * N_PAIRS, N_SEQ, LANES), F32)],
        compiler_params=_cparams(("arbitrary",)),
        name="ssm",
    )(u_all, u_all, ssm["wb"], ssm["wc"], ssm["a_re"], ssm["a_im"], ssm["jrev"])


def _layer_norm(v, g, b):
    mu = jnp.mean(v, -1, keepdims=True)
    var = jnp.mean(jnp.square(v - mu), -1, keepdims=True)
    return (v - mu) * lax.rsqrt(var + EPS) * g + b


def _gelu_tanh(v):
    return 0.5 * v * (1.0 + jnp.tanh(math.sqrt(2.0 / math.pi) * (v + 0.044715 * (v * v * v))))


def _route(logits):
    lane = lax.broadcasted_iota(jnp.int32, logits.shape, 1).astype(F32)
    neg = jnp.float32(-jnp.inf)
    big = jnp.float32(1 << 20)
    gl = jnp.where((lane >= N_EXPERTS) & (lane < N_EXPERTS + N_EGROUPS), logits, neg)
    gmax = jnp.max(gl, -1, keepdims=True)
    g_w = 1.0 / jnp.sum(jnp.exp(gl - gmax), -1, keepdims=True)
    g_lane = jnp.min(jnp.where(gl == gmax, lane, big), -1, keepdims=True)
    lo = (g_lane - N_EXPERTS) * E_PER_GROUP
    el = jnp.where((lane >= lo) & (lane < lo + E_PER_GROUP), logits, neg)
    m1 = jnp.max(el, -1, keepdims=True)
    i1 = jnp.min(jnp.where(el == m1, lane, big), -1, keepdims=True)
    el2 = jnp.where(lane == i1, neg, el)
    m2 = jnp.max(el2, -1, keepdims=True)
    i2 = jnp.min(jnp.where(el2 == m2, lane, big), -1, keepdims=True)
    esum = jnp.sum(jnp.exp(el - m1), -1, keepdims=True)
    p1 = 1.0 / esum
    p2 = jnp.exp(m2 - m1) / esum
    w1 = p1 / (p1 + p2) * g_w
    w2 = p2 / (p1 + p2) * g_w
    onehot = jnp.where((lane == i1) | (lane == i2 + N_EXPERTS), 1.0, 0.0)
    info = jnp.where(lane == 0, i1, jnp.where(lane == 1, i2, jnp.where(lane == 2, w1, jnp.where(lane == 3, w2, 0.0))))
    return onehot, info


def _merge_kernel(at_ref, y0_ref, y1_ref, u_ref, x_ref, mod_ref, dsk_ref, wglu_ref, bglu_ref, gna_ref, gns_ref,
                  wo_ref, ln1g_ref, ln1b_ref, wrh_ref, wrl_ref, br_ref, x1_ref, hp_ref, oh_ref, info_ref, hs_ref):
    b = pl.program_id(0)
    mod = mod_ref[pl.ds(b, 1), :]
    gate_a = mod[:, 2 * D_MODEL:3 * D_MODEL]
    shift_f = mod[:, 3 * D_MODEL:4 * D_MODEL]
    scale_f = mod[:, 4 * D_MODEL:5 * D_MODEL]

    g = _gelu_tanh(y0_ref[0] + y1_ref[0] + dsk_ref[...] * u_ref[0])
    z = jnp.dot(g.astype(BF16), wglu_ref[...], preferred_element_type=F32) + bglu_ref[...]
    ssm = g * jax.nn.sigmoid(z)
    attn = at_ref[0].T
    an = _rms(attn, gna_ref[...]).astype(BF16)
    sn = _rms(ssm, gns_ref[...]).astype(BF16)
    mix = (jnp.dot(an, wo_ref[0:ATTN_W, :], preferred_element_type=F32)
           + jnp.dot(sn, wo_ref[ATTN_W:ATTN_W + SSM_W, :], preferred_element_type=F32))
    x1 = _layer_norm(ALPHA * x_ref[0] + gate_a * mix, ln1g_ref[...], ln1b_ref[...])
    x1_ref[0] = x1
    h = x1 * (1.0 + scale_f) + shift_f
    h_hi = h.astype(BF16)
    h_hif = h_hi.astype(F32)
    h_lo = (h - h_hif).astype(BF16)
    logits = (jnp.dot(h_hi, wrh_ref[...], preferred_element_type=F32)
              + jnp.dot(h_lo, wrh_ref[...], preferred_element_type=F32)
              + jnp.dot(h_hi, wrl_ref[...], preferred_element_type=F32)) + br_ref[...]
    onehot, info = _route(logits)
    oh_ref[...] = onehot.astype(BF16)
    info_ref[...] = info
    t = h.shape[0]
    for c in range(SLAB_ROWS):
        hs_ref[pl.ds(c, t, stride=SLAB_ROWS), :] = h[:, c * LANES:(c + 1) * LANES]
    hp_ref[...] = hs_ref[...].astype(BF16)


def _merge(attn_t, y0, y1, u_all, x, mod, prep, ctx_len):
    bsz, n, _ = x.shape
    off = ctx_len // ROW_TILE
    tok = lambda w: pl.BlockSpec((1, ROW_TILE, w), lambda b, t: (b, t, 0))
    tok_off = lambda w: pl.BlockSpec((1, ROW_TILE, w), lambda b, t: (b, t + off, 0))
    weights = [prep["d_skip"], prep["w_glu"], prep["b_glu"], prep["gn_attn"], prep["gn_ssm"], prep["w_o"],
               prep["ln1_g"], prep["ln1_b"], prep["w_router_hi"], prep["w_router_lo"], prep["b_router"]]
    nt = n // ROW_TILE
    flat = lambda rows: pl.BlockSpec((rows, LANES), lambda b, t: (b * nt + t, 0))
    return pl.pallas_call(
        _merge_kernel,
        grid=(bsz, nt),
        in_specs=[pl.BlockSpec((1, ATTN_W, ROW_TILE), lambda b, t: (b, 0, t)),
                  tok_off(SSM_W), tok_off(SSM_W), tok_off(SSM_W), tok(D_MODEL), _full(mod.shape)]
                 + [_full(w.shape) for w in weights],
        out_specs=[tok(D_MODEL), flat(ROW_TILE * SLAB_ROWS), flat(ROW_TILE), flat(ROW_TILE)],
        out_shape=[jax.ShapeDtypeStruct((bsz, n, D_MODEL), F32),
                   jax.ShapeDtypeStruct((bsz * n * SLAB_ROWS, LANES), BF16),
                   jax.ShapeDtypeStruct((bsz * n, LANES), BF16),
                   jax.ShapeDtypeStruct((bsz * n, LANES), F32)],
        scratch_shapes=[pltpu.VMEM((ROW_TILE * SLAB_ROWS, LANES), F32)],
        compiler_params=_cparams(("arbitrary", "arbitrary")),
        name="merge",
    )(attn_t, y0, y1, u_all, x, mod, *weights)


def _rank_kernel(oh_ref, tri_ref, rank_ref, cnt_ref, carry_ref):
    @pl.when(pl.program_id(0) == 0)
    def _():
        carry_ref[...] = jnp.zeros_like(carry_ref)

    oh = oh_ref[...]
    ohf = oh.astype(F32)
    before = jnp.dot(tri_ref[...], oh, preferred_element_type=F32) + carry_ref[0:1, :]
    lane = lax.broadcasted_iota(jnp.int32, ohf.shape, 1)
    hit = ohf * before
    r1 = jnp.sum(jnp.where(lane < N_EXPERTS, hit, 0.0), -1, keepdims=True)
    r2 = jnp.sum(jnp.where(lane >= N_EXPERTS, hit, 0.0), -1, keepdims=True)
    rank_ref[...] = jnp.where(lane == 0, r1, jnp.where(lane == 1, r2, 0.0))
    total = carry_ref[...] + jnp.sum(ohf, 0, keepdims=True)
    carry_ref[...] = total
    cnt_ref[...] = total


def _rank(onehot):
    rows = onehot.shape[0]
    tile = min(RANK_TILE, rows)
    tri = jnp.tril(jnp.ones((tile, tile), F32), -1).astype(BF16)
    return pl.pallas_call(
        _rank_kernel,
        grid=(rows // tile,),
        in_specs=[pl.BlockSpec((tile, LANES), lambda i: (i, 0)), _full(tri.shape)],
        out_specs=[pl.BlockSpec((tile, LANES), lambda i: (i, 0)), _full((8, LANES))],
        out_shape=[jax.ShapeDtypeStruct((rows, LANES), F32), jax.ShapeDtypeStruct((8, LANES), F32)],
        scratch_shapes=[pltpu.VMEM((8, LANES), F32)],
        compiler_params=_cparams(("arbitrary",)),
        name="rank",
    )(onehot, tri)


def _expert_kernel(n_tok, te_ref, tn_ref, src_ref, hp_hbm, wg_ref, wu_ref, wd_ref, y_hbm,
                   hres_ref, xbuf_ref, yslab_ref, pend_ref, load_sem, out_sem):
    i = pl.program_id(0)
    last = pl.num_programs(0) - 1
    slot = i % 2
    tm = EXPERT_TILE
    n_valid = tn_ref[i]

    def wait_rows(s):
        rows = pend_ref[s] * SLAB_ROWS
        pltpu.make_async_copy(yslab_ref.at[s, pl.ds(0, rows), :], y_hbm.at[pl.ds(0, rows), :], out_sem.at[s]).wait()
        pend_ref[s] = 0

    def row_copy(r, a):
        return pltpu.make_async_copy(yslab_ref.at[slot, pl.ds(pl.multiple_of(r * SLAB_ROWS, SLAB_ROWS), SLAB_ROWS), :],
                                     y_hbm.at[pl.ds(pl.multiple_of(a * SLAB_ROWS, SLAB_ROWS), SLAB_ROWS), :],
                                     out_sem.at[slot])

    @pl.when(i == 0)
    def _():
        pend_ref[0] = 0
        pend_ref[1] = 0
        cp = pltpu.make_async_copy(hp_hbm, hres_ref, load_sem)
        cp.start()
        cp.wait()

    @pl.when(n_valid > 0)
    def _():
        @pl.when(pend_ref[slot] > 0)
        def _():
            wait_rows(slot)

        for r in range(tm):
            tok = src_ref[0, 0, r] & (n_tok - 1)
            pair = hres_ref[pl.ds(pl.multiple_of((tok >> 1) * (2 * SLAB_ROWS), 2 * SLAB_ROWS), 2 * SLAB_ROWS), :]
            pair = pair.astype(F32)
            xbuf_ref[pl.ds(r, SLAB_ROWS, stride=XBUF_PITCH), :] = jnp.where(
                (tok & 1) == 1, pair[SLAB_ROWS:2 * SLAB_ROWS], pair[0:SLAB_ROWS])
        x = jnp.concatenate([xbuf_ref[c * XBUF_PITCH:c * XBUF_PITCH + tm, :] for c in range(SLAB_ROWS)],
                            axis=-1).astype(BF16)
        a = jnp.dot(x, wg_ref[0].astype(BF16), preferred_element_type=F32)
        u = jnp.dot(x, wu_ref[0].astype(BF16), preferred_element_type=F32)
        act = (a * jax.nn.sigmoid(a) * u).astype(BF16)
        y = jnp.dot(act, wd_ref[0].astype(BF16), preferred_element_type=F32)
        for c in range(SLAB_ROWS):
            yslab_ref[slot, pl.ds(c, tm, stride=SLAB_ROWS), :] = y[:, c * LANES:(c + 1) * LANES]

        @pl.when(n_valid == tm)
        def _():
            for r in range(tm):
                row_copy(r, src_ref[0, 0, r]).start()

        @pl.when(n_valid < tm)
        def _():
            def issue(r, carry):
                row_copy(r, src_ref[0, 0, r]).start()
                return carry
            lax.fori_loop(0, n_valid, issue, 0)

        pend_ref[slot] = n_valid

    @pl.when(i == last)
    def _():
        for s in range(2):
            @pl.when(pend_ref[s] > 0)
            def _():
                wait_rows(s)


def _experts(tile_e, tile_n, src, hp, wg, wu, wd, n_tok):
    n_tiles = tile_e.shape[0]
    tm = EXPERT_TILE
    assert n_tok & (n_tok - 1) == 0
    wspec = lambda shape: pl.BlockSpec((1,) + shape, lambda i, te, tn: (te[i], 0, 0))
    grid_spec = pltpu.PrefetchScalarGridSpec(
        num_scalar_prefetch=2,
        grid=(n_tiles,),
        in_specs=[pl.BlockSpec((1, 1, tm), lambda i, te, tn: (i, 0, 0), memory_space=pltpu.SMEM),
                  pl.BlockSpec(memory_space=pl.ANY),
                  wspec((D_MODEL, D_FF)), wspec((D_MODEL, D_FF)), wspec((D_FF, D_MODEL))],
        out_specs=pl.BlockSpec(memory_space=pl.ANY),
        scratch_shapes=[pltpu.VMEM(hp.shape, BF16),
                        pltpu.VMEM((SLAB_ROWS * XBUF_PITCH, LANES), F32),
                        pltpu.VMEM((2, tm * SLAB_ROWS, LANES), F32),
                        pltpu.SMEM((2,), jnp.int32),
                        pltpu.SemaphoreType.DMA(()),
                        pltpu.SemaphoreType.DMA((2,))])
    return pl.pallas_call(
        functools.partial(_expert_kernel, n_tok),
        grid_spec=grid_spec,
        out_shape=jax.ShapeDtypeStruct((2 * n_tok * SLAB_ROWS, LANES), F32),
        compiler_params=pltpu.CompilerParams(dimension_semantics=("arbitrary",),
                                             vmem_limit_bytes=EXPERT_VMEM_LIMIT),
        name="experts",
    )(tile_e, tile_n, src.reshape(n_tiles, 1, tm), hp, wg, wu, wd)


def _combine_kernel(y0_ref, y1_ref, info_ref, x1_ref, mod_ref, ln2g_ref, ln2b_ref, o_ref):
    b = pl.program_id(0)
    gate_f = mod_ref[pl.ds(b, 1), 5 * D_MODEL:6 * D_MODEL]
    t = x1_ref.shape[1]
    info = info_ref[...]
    w1, w2 = info[:, 2:3], info[:, 3:4]
    ffn = jnp.concatenate([w1 * y0_ref[pl.ds(c, t, stride=SLAB_ROWS), :] + w2 * y1_ref[pl.ds(c, t, stride=SLAB_ROWS), :]
                           for c in range(SLAB_ROWS)], axis=-1)
    o_ref[0] = _layer_norm(ALPHA * x1_ref[0] + gate_f * ffn, ln2g_ref[...], ln2b_ref[...])


def _combine(y_slots, info, x1, mod, ln2_g, ln2_b):
    bsz, n, _ = x1.shape
    nt = n // ROW_TILE
    slab = lambda k: pl.BlockSpec((ROW_TILE * SLAB_ROWS, LANES), lambda b, t: (k * bsz * nt + b * nt + t, 0))
    return pl.pallas_call(
        _combine_kernel,
        grid=(bsz, nt),
        in_specs=[slab(0), slab(1), pl.BlockSpec((ROW_TILE, LANES), lambda b, t: (b * nt + t, 0)),
                  pl.BlockSpec((1, ROW_TILE, D_MODEL), lambda b, t: (b, t, 0)),
                  _full(mod.shape), _full(ln2_g.shape), _full(ln2_b.shape)],
        out_specs=pl.BlockSpec((1, ROW_TILE, D_MODEL), lambda b, t: (b, t, 0)),
        out_shape=jax.ShapeDtypeStruct((bsz, n, D_MODEL), F32),
        compiler_params=_cparams(("arbitrary", "arbitrary")),
        name="combine",
    )(y_slots, y_slots, info, x1, mod, ln2_g, ln2_b)


def _invert_kernel(n_tok, n_tiles, cnt_ref, e1_ref, e2_ref, r1_ref, r2_ref, src_ref, te_ref, tn_ref, base_ref):
    step = pl.program_id(0)
    tm = EXPERT_TILE
    chunk = e1_ref.shape[2]

    @pl.when(step == 0)
    def _():
        def fill(p, c):
            src_ref[p] = 0
            return c
        lax.fori_loop(0, n_tiles * tm, fill, 0, unroll=8)

        def per_expert(e, carry):
            tile0, last_e = carry
            c1 = cnt_ref[e]
            cnt = c1 + cnt_ref[N_EXPERTS + e]
            nt = (cnt + (tm - 1)) // tm
            base_ref[e] = tile0 * tm
            base_ref[N_EXPERTS + e] = tile0 * tm + c1

            def per_tile(j, c):
                te_ref[tile0 + j] = e
                tn_ref[tile0 + j] = jnp.minimum(cnt - j * tm, tm)
                return c
            lax.fori_loop(0, nt, per_tile, 0)
            return tile0 + nt, jnp.where(nt > 0, e, last_e)

        used, last_e = lax.fori_loop(0, N_EXPERTS, per_expert, (jnp.int32(0), jnp.int32(0)))

        def idle(j, c):
            te_ref[j] = last_e
            tn_ref[j] = 0
            return c
        lax.fori_loop(used, n_tiles, idle, 0)

    def body(j, c):
        t = step * chunk + j
        src_ref[base_ref[e1_ref[0, 0, j]] + r1_ref[0, 0, j]] = t
        src_ref[base_ref[N_EXPERTS + e2_ref[0, 0, j]] + r2_ref[0, 0, j]] = n_tok + t
        return c
    lax.fori_loop(0, chunk, body, 0, unroll=8)


def _invert(info, rank, counts, n_tok):
    tm = EXPERT_TILE
    n_tiles = 2 * n_tok // tm + N_EXPERTS
    chunk = min(INVERT_CHUNK, n_tok)
    nc = n_tok // chunk
    col = lambda a, j: a[:, j].astype(jnp.int32).reshape(nc, 1, chunk)
    cnt = counts[0, 0:2 * N_EXPERTS].astype(jnp.int32)
    idx_spec = pl.BlockSpec((1, 1, chunk), lambda i, c: (i, 0, 0), memory_space=pltpu.SMEM)
    smem_out = pl.BlockSpec(memory_space=pltpu.SMEM)
    grid_spec = pltpu.PrefetchScalarGridSpec(
        num_scalar_prefetch=1, grid=(nc,),
        in_specs=[idx_spec] * 4,
        out_specs=[smem_out] * 3,
        scratch_shapes=[pltpu.SMEM((2 * N_EXPERTS,), jnp.int32)])
    return pl.pallas_call(
        functools.partial(_invert_kernel, n_tok, n_tiles),
        grid_spec=grid_spec,
        out_shape=[jax.ShapeDtypeStruct((n_tiles * tm,), jnp.int32),
                   jax.ShapeDtypeStruct((n_tiles,), jnp.int32),
                   jax.ShapeDtypeStruct((n_tiles,), jnp.int32)],
        compiler_params=_cparams(("arbitrary",)),
        name="invert",
    )(cnt, col(info, 0), col(info, 1), col(rank, 0), col(rank, 1))


def _rope_tables(n):
    rows = n // GRID_W
    row = jnp.repeat(jnp.arange(rows), GRID_W).astype(F32)
    col = jnp.tile(jnp.arange(GRID_W), rows).astype(F32)
    axis_dim = QK_ROPE // 2
    inv_freq = ROPE_BASE ** (-jnp.arange(0, axis_dim, 2, dtype=F32) / axis_dim)
    ang = jnp.concatenate([row[:, None] * inv_freq, col[:, None] * inv_freq], -1)
    ang = jnp.concatenate([ang, ang], -1)
    pad_l = jnp.ones((n, QK_NOPE), F32)
    pad_r = jnp.ones((n, HEAD_PAD - QK_NOPE - QK_ROPE), F32)
    cos_t = jnp.concatenate([pad_l, jnp.cos(ang), pad_r], -1)
    sin_t = jnp.concatenate([0 * pad_l, jnp.sin(ang), 0 * pad_r], -1)
    return cos_t, sin_t


def _rot_half_cols(w):
    w1, w2 = w[..., :QK_ROPE // 2], w[..., QK_ROPE // 2:]
    return jnp.concatenate([-w2, w1], -1)


def _prep_project(w_in, q_norm_g, kv_norm_g, w_uq, w_ukv):
    zeros = lambda r, c: jnp.zeros((r, c), F32)
    w_in_p = jnp.concatenate([w_in[:, :MLA_IN], zeros(D_MODEL, 512 - MLA_IN), w_in[:, MLA_IN:]], -1)
    qk_scale = SM_SCALE * math.log2(math.e)
    wq = (w_uq * qk_scale).reshape(Q_RANK, N_HEADS, QK_NOPE + QK_ROPE)
    pad = jnp.zeros((Q_RANK, N_HEADS, HEAD_PAD - QK_NOPE - QK_ROPE), F32)
    wq_pad = jnp.concatenate([wq, pad], -1).reshape(Q_RANK, N_HEADS * HEAD_PAD)
    wq_rot = jnp.concatenate([0 * wq[..., :QK_NOPE], _rot_half_cols(wq[..., QK_NOPE:]), pad], -1)
    wq_rot = wq_rot.reshape(Q_RANK, N_HEADS * HEAD_PAD)
    wkv = w_ukv.reshape(KV_RANK, N_HEADS, QK_NOPE + V_DIM)
    wk = jnp.concatenate([wkv[..., :QK_NOPE], jnp.zeros((KV_RANK, N_HEADS, HEAD_PAD - QK_NOPE), F32)], -1)
    wk = wk.reshape(KV_RANK, N_HEADS * HEAD_PAD)
    wvt = wkv[..., QK_NOPE:].reshape(KV_RANK, ATTN_W).T
    eye = jnp.eye(QK_ROPE, dtype=F32)
    place = lambda m: jnp.zeros((LANES, HEAD_PAD), F32).at[:QK_ROPE, QK_NOPE:QK_NOPE + QK_ROPE].set(m)
    return dict(w_in=w_in_p.astype(BF16), q_g=q_norm_g.reshape(1, -1), kv_g=kv_norm_g.reshape(1, -1),
                wq=wq_pad.astype(BF16), wq_rot=wq_rot.astype(BF16), wk=wk.astype(BF16),
                wvt=wvt.astype(BF16), e=place(eye).astype(BF16), e_rot=place(_rot_half_cols(eye)).astype(BF16))


def _discretise(a_re, a_im, log_dt, b_re, b_im):
    dt = jnp.exp(log_dt)[:, None]
    mag = jnp.exp(a_re * dt)
    abar_re, abar_im = mag * jnp.cos(a_im * dt), mag * jnp.sin(a_im * dt)
    den = jnp.square(a_re) + jnp.square(a_im)
    num_re = abar_re - 1.0
    coef_re = (num_re * a_re + abar_im * a_im) / den
    coef_im = (abar_im * a_re - num_re * a_im) / den
    bbar_re = coef_re[..., None] * b_re - coef_im[..., None] * b_im
    bbar_im = coef_re[..., None] * b_im + coef_im[..., None] * b_re
    return abar_re, abar_im, bbar_re, bbar_im


def _prep_ssm(a_re, a_im, log_dt, b_re, b_im, c_re, c_im, bsz):
    wbs, wcs, ares, aims = [], [], [], []
    eye2 = jnp.eye(2, dtype=F32)
    quarter = jax.nn.one_hot(jnp.arange(N_PAIRS) % 4, 4, dtype=F32)
    for d in range(2):
        abar_re, abar_im, bbar_re, bbar_im = _discretise(a_re[d], a_im[d], log_dt[d], b_re[d], b_im[d])

        def in_block(bb):
            t = bb.reshape(N_PAIRS, 2, SSM_STATE, SSM_GROUP)
            return jnp.einsum('jgph,gk->jghkp', t, eye2).reshape(N_PAIRS, 2 * SSM_GROUP, 2 * SSM_STATE)

        blk = jnp.concatenate([in_block(bbar_re), in_block(bbar_im)], -1)
        wbs.append(jnp.einsum('jrc,jq->jqrc', blk, quarter).reshape(N_PAIRS, LANES, 2 * LANES))

        def out_block(cc):
            t = cc.reshape(N_PAIRS, 2, SSM_GROUP, SSM_STATE)
            return jnp.einsum('jghp,gk->jgpkh', t, eye2).reshape(N_PAIRS, 2 * SSM_STATE, 2 * SSM_GROUP)

        oblk = jnp.concatenate([out_block(c_re[d]), -out_block(c_im[d])], 1)
        wcs.append(jnp.einsum('jkc,jq->jkqc', oblk, quarter).reshape(N_PAIRS, 2 * LANES, LANES))
        ares.append(jnp.broadcast_to(abar_re.reshape(N_PAIRS, 1, LANES), (N_PAIRS, bsz, LANES)))
        aims.append(jnp.broadcast_to(abar_im.reshape(N_PAIRS, 1, LANES), (N_PAIRS, bsz, LANES)))
    jrev = jnp.eye(SSM_CHUNK, dtype=F32)[::-1]
    return dict(wb=jnp.stack(wbs).astype(BF16), wc=jnp.stack(wcs).astype(BF16),
                a_re=jnp.concatenate(ares, 1), a_im=jnp.concatenate(aims, 1), jrev=jrev.astype(BF16))


def kernel(x, c, ctx, c_ctx, w_ada, b_ada, w_in, q_norm_g, kv_norm_g, w_uq, w_ukv, ssm_a_re, ssm_a_im,
           ssm_log_dt, ssm_b_re, ssm_b_im, ssm_c_re, ssm_c_im, ssm_d, w_glu, b_glu, gn_attn_g, gn_ssm_g, w_o,
           ln1_g, ln1_b, w_router_group, b_router_group, w_router_expert, b_router_expert, w_exp_gate,
           w_exp_up, w_exp_down, ln2_g, ln2_b):
    bsz, n, _ = x.shape
    ctx_len = ctx.shape[1]
    n_all = ctx_len + n
    l = 0
    row = lambda v: v.reshape(1, -1)

    cvec = jnp.concatenate([c, c_ctx[None, :], jnp.zeros((8 - bsz - 1, D_MODEL), F32)], 0)
    mod = _adaln(cvec, w_ada[l], b_ada[l])

    prep = _prep_project(w_in[l], q_norm_g[l], kv_norm_g[l], w_uq[l], w_ukv[l])
    cos_t, sin_t = _rope_tables(n)
    ones_t = jnp.ones((ctx_len, HEAD_PAD), F32)
    q, k, vt, u_all = _project(x, mod, cos_t, sin_t, prep, n_all, ctx_len)
    k, vt, u_all = _project(ctx, mod, ones_t, 0 * ones_t, prep, n_all, 0, prev=(k, vt, u_all))

    attn_t = _attention(q, k, vt)

    ssm = _prep_ssm(ssm_a_re[l], ssm_a_im[l], ssm_log_dt[l], ssm_b_re[l], ssm_b_im[l],
                    ssm_c_re[l], ssm_c_im[l], bsz)
    y0, y1 = _ssm(u_all, ssm)

    w_router = jnp.concatenate([w_router_expert[l], w_router_group[l],
                                jnp.zeros((D_MODEL, LANES - N_EXPERTS - N_EGROUPS), F32)], -1)
    b_router = jnp.concatenate([b_router_expert[l], b_router_group[l],
                                jnp.zeros((LANES - N_EXPERTS - N_EGROUPS,), F32)])
    w_router_hi = w_router.astype(BF16)
    mprep = dict(d_skip=row(ssm_d[l]), w_glu=w_glu[l].astype(BF16), b_glu=row(b_glu[l]),
                 gn_attn=row(gn_attn_g[l]), gn_ssm=row(gn_ssm_g[l]), w_o=w_o[l].astype(BF16),
                 ln1_g=row(ln1_g[l]), ln1_b=row(ln1_b[l]), w_router_hi=w_router_hi,
                 w_router_lo=(w_router - w_router_hi.astype(F32)).astype(BF16), b_router=row(b_router))
    x1, hp, onehot, info = _merge(attn_t, y0, y1, u_all, x, mod, mprep, ctx_len)

    rank, counts = _rank(onehot)
    src, tile_e, tile_n = _invert(info, rank, counts, bsz * n)
    y_slots = _experts(tile_e, tile_n, src, hp, w_exp_gate[l], w_exp_up[l], w_exp_down[l], bsz * n)
    return _combine(y_slots, info, x1, mod, row(ln2_g[l]), row(ln2_b[l]))
```

```python
import functools
import math

import jax
import jax.numpy as jnp
from jax import lax
from jax.experimental import pallas as pl
from jax.experimental.pallas import tpu as pltpu

F32 = jnp.float32
BF16 = jnp.bfloat16

D_MODEL = 1024
GRID_W = 64
N_HEADS = 8
QK_NOPE = 64
QK_ROPE = 32
V_DIM = 64
Q_RANK = 256
KV_RANK = 128
ATTN_W = N_HEADS * V_DIM
MLA_IN = Q_RANK + KV_RANK + QK_ROPE
ROPE_BASE = 10000.0
SM_SCALE = (QK_NOPE + QK_ROPE) ** -0.5
SSM_W = D_MODEL - ATTN_W
SSM_GROUP = 16
N_GROUPS = SSM_W // SSM_GROUP
SSM_STATE = 64
N_EGROUPS = 4
E_PER_GROUP = 8
N_EXPERTS = N_EGROUPS * E_PER_GROUP
D_FF = 256
DEPTH = 1
ALPHA = (2 * DEPTH) ** 0.25
EPS = 1e-6

LANES = 128
HEAD_PAD = LANES
VMEM_LIMIT = 48 * 1024 * 1024

ROW_TILE = 256
Q_TILE = 256
KV_CHUNKS = 2
SSM_CHUNK = 128
SSM_SUB = SSM_CHUNK // 2
N_SEQ = 8
N_PAIRS = N_GROUPS // 2
RANK_TILE = 512
EXPERT_TILE = 256
INVERT_CHUNK = 2048
SLAB_ROWS = D_MODEL // LANES
XBUF_PITCH = EXPERT_TILE + 8
EXPERT_VMEM_LIMIT = 56 * 1024 * 1024


def _cparams(sem):
    return pltpu.CompilerParams(dimension_semantics=sem, vmem_limit_bytes=VMEM_LIMIT)


def _full(shape):
    return pl.BlockSpec(shape, lambda *_: (0,) * len(shape))


def _adaln_kernel(c_ref, w_ref, b_ref, o_ref):
    c = c_ref[...]
    s = c * jax.nn.sigmoid(c)
    o_ref[...] = jnp.dot(s, w_ref[...], preferred_element_type=F32,
                         precision=lax.Precision.HIGHEST) + b_ref[...]


def _adaln(cvec, w_ada, b_ada):
    n_out = w_ada.shape[1]
    tn = 1536
    return pl.pallas_call(
        _adaln_kernel,
        grid=(n_out // tn,),
        in_specs=[_full(cvec.shape),
                  pl.BlockSpec((D_MODEL, tn), lambda j: (0, j)),
                  pl.BlockSpec((1, tn), lambda j: (0, j))],
        out_specs=pl.BlockSpec((cvec.shape[0], tn), lambda j: (0, j)),
        out_shape=jax.ShapeDtypeStruct((cvec.shape[0], n_out), F32),
        compiler_params=_cparams(("arbitrary",)),
        name="adaln",
    )(cvec, w_ada, b_ada.reshape(1, n_out))


def _rms(v, g):
    return v * lax.rsqrt(jnp.mean(jnp.square(v), -1, keepdims=True) + EPS) * g


def _project_kernel(with_q, mod_row_fn, x_ref, mod_ref, cos_ref, sin_ref, w_in_ref, qg_ref, kvg_ref,
                    wq_ref, wqr_ref, wk_ref, wvt_ref, e_ref, er_ref, *rest):
    if with_q:
        q_ref, k_ref, vt_ref, u_ref = rest
    else:
        k_ref, vt_ref, u_ref = rest[-3:]
    row = mod_row_fn(pl.program_id(0))
    mod = mod_ref[pl.ds(row, 1), :]
    shift = mod[:, 0:D_MODEL]
    scale = mod[:, D_MODEL:2 * D_MODEL]
    xm = (x_ref[0] * (1.0 + scale) + shift).astype(BF16)
    p = jnp.dot(xm, w_in_ref[...], preferred_element_type=F32)
    u_ref[0] = p[:, 512:1024]
    cos = cos_ref[...]
    sin = sin_ref[...]

    ckvn = _rms(p[:, Q_RANK:Q_RANK + KV_RANK], kvg_ref[...]).astype(BF16)
    kn = jnp.dot(ckvn, wk_ref[...], preferred_element_type=F32)
    kr = p[:, 384:512].astype(BF16)
    krope = (jnp.dot(kr, e_ref[...], preferred_element_type=F32) * cos
             + jnp.dot(kr, er_ref[...], preferred_element_type=F32) * sin)
    for h in range(N_HEADS):
        k_ref[0, h] = (kn[:, h * HEAD_PAD:(h + 1) * HEAD_PAD] + krope).astype(BF16)
    vt_ref[0] = lax.dot_general(wvt_ref[...], ckvn, (((1,), (1,)), ((), ())),
                                preferred_element_type=F32).astype(BF16)

    if with_q:
        qn = _rms(p[:, 0:Q_RANK], qg_ref[...]).astype(BF16)
        qa = jnp.dot(qn, wq_ref[...], preferred_element_type=F32)
        qb = jnp.dot(qn, wqr_ref[...], preferred_element_type=F32)
        for h in range(N_HEADS):
            sl = slice(h * HEAD_PAD, (h + 1) * HEAD_PAD)
            q_ref[0, h] = (qa[:, sl] * cos + qb[:, sl] * sin).astype(BF16)


def _project(x, mod, cos_t, sin_t, prep, n_all, row_off, prev=None):
    bsz, n, _ = x.shape
    with_q = prev is None
    nt = n // ROW_TILE
    off = row_off // ROW_TILE
    mod_row_fn = (lambda b: b) if with_q else (lambda b: bsz)
    w_specs = [_full(prep["w_in"].shape), _full(prep["q_g"].shape), _full(prep["kv_g"].shape),
               _full(prep["wq"].shape), _full(prep["wq_rot"].shape), _full(prep["wk"].shape),
               _full(prep["wvt"].shape), _full(prep["e"].shape), _full(prep["e_rot"].shape)]
    in_specs = [pl.BlockSpec((1, ROW_TILE, D_MODEL), lambda b, t: (b, t, 0)),
                _full(mod.shape),
                pl.BlockSpec((ROW_TILE, HEAD_PAD), lambda b, t: (t, 0)),
                pl.BlockSpec((ROW_TILE, HEAD_PAD), lambda b, t: (t, 0))] + w_specs
    k_spec = pl.BlockSpec((1, N_HEADS, ROW_TILE, HEAD_PAD), lambda b, t: (b, 0, t + off, 0))
    vt_spec = pl.BlockSpec((1, ATTN_W, ROW_TILE), lambda b, t: (b, 0, t + off))
    u_spec = pl.BlockSpec((1, ROW_TILE, SSM_W), lambda b, t: (b, t + off, 0))
    k_shape = jax.ShapeDtypeStruct((bsz, N_HEADS, n_all, HEAD_PAD), BF16)
    vt_shape = jax.ShapeDtypeStruct((bsz, ATTN_W, n_all), BF16)
    u_shape = jax.ShapeDtypeStruct((bsz, n_all, SSM_W), F32)
    args = [x, mod, cos_t, sin_t, prep["w_in"], prep["q_g"], prep["kv_g"], prep["wq"], prep["wq_rot"],
            prep["wk"], prep["wvt"], prep["e"], prep["e_rot"]]
    if with_q:
        out_specs = [pl.BlockSpec((1, N_HEADS, ROW_TILE, HEAD_PAD), lambda b, t: (b, 0, t, 0)),
                     k_spec, vt_spec, u_spec]
        out_shape = [jax.ShapeDtypeStruct((bsz, N_HEADS, n, HEAD_PAD), BF16), k_shape, vt_shape, u_shape]
        aliases = {}
    else:
        k_prev, vt_prev, u_prev = prev
        in_specs += [pl.BlockSpec(memory_space=pl.ANY)] * 3
        args += [k_prev, vt_prev, u_prev]
        out_specs = [k_spec, vt_spec, u_spec]
        out_shape = [k_shape, vt_shape, u_shape]
        aliases = {13: 0, 14: 1, 15: 2}
    return pl.pallas_call(
        functools.partial(_project_kernel, with_q, mod_row_fn), grid=(bsz, nt), in_specs=in_specs, out_specs=out_specs, out_shape=out_shape,
        input_output_aliases=aliases,
        compiler_params=_cparams(("arbitrary", "arbitrary")),
        name="project_lat" if with_q else "project_ctx",
    )(*args)


def _attn_kernel(n_chunks, q_ref, k_ref, vt_ref, o_ref, sa_ref, sb_ref):
    kv_chunk = k_ref.shape[2] // n_chunks

    def scores(h, s_ref):
        q = q_ref[0, h]
        m = None
        for c in range(n_chunks):
            kc = k_ref[0, h, pl.ds(c * kv_chunk, kv_chunk), :]
            s = lax.dot_general(kc, q, (((1,), (1,)), ((), ())), preferred_element_type=F32)
            s_ref[pl.ds(c * kv_chunk, kv_chunk), :] = s
            cm = jnp.max(s, axis=0, keepdims=True)
            m = cm if m is None else jnp.maximum(m, cm)
        return m

    def softmax_pv(h, s_ref, m):
        l = jnp.zeros_like(m)
        acc = jnp.zeros((V_DIM, m.shape[1]), F32)
        row = pl.ds(pl.multiple_of(h * V_DIM, V_DIM), V_DIM)
        for c in range(n_chunks):
            p = jnp.exp2(s_ref[pl.ds(c * kv_chunk, kv_chunk), :] - m)
            l = l + jnp.sum(p, axis=0, keepdims=True)
            vc = vt_ref[0, row, pl.ds(c * kv_chunk, kv_chunk)]
            acc = acc + jnp.dot(vc, p.astype(BF16), preferred_element_type=F32)
        o_ref[0, row, :] = acc / l

    def pair(j, m_even):
        h = 2 * j
        m_odd = scores(h + 1, sb_ref)
        softmax_pv(h, sa_ref, m_even)
        m_next = scores(h + 2, sa_ref)
        softmax_pv(h + 1, sb_ref, m_odd)
        return m_next

    m6 = lax.fori_loop(0, N_HEADS // 2 - 1, pair, scores(0, sa_ref))
    m7 = scores(N_HEADS - 1, sb_ref)
    softmax_pv(N_HEADS - 2, sa_ref, m6)
    softmax_pv(N_HEADS - 1, sb_ref, m7)


def _attention(q, k, vt):
    bsz, _, n, _ = q.shape
    n_all = k.shape[2]
    n_chunks = KV_CHUNKS
    assert n_all % (n_chunks * LANES) == 0
    return pl.pallas_call(
        functools.partial(_attn_kernel, n_chunks),
        grid=(bsz, n // Q_TILE),
        in_specs=[pl.BlockSpec((1, N_HEADS, Q_TILE, HEAD_PAD), lambda b, t: (b, 0, t, 0)),
                  pl.BlockSpec((1, N_HEADS, n_all, HEAD_PAD), lambda b, t: (b, 0, 0, 0)),
                  pl.BlockSpec((1, ATTN_W, n_all), lambda b, t: (b, 0, 0))],
        out_specs=pl.BlockSpec((1, ATTN_W, Q_TILE), lambda b, t: (b, 0, t)),
        out_shape=jax.ShapeDtypeStruct((bsz, ATTN_W, n), F32),
        scratch_shapes=[pltpu.VMEM((n_all, Q_TILE), F32), pltpu.VMEM((n_all, Q_TILE), F32)],
        compiler_params=_cparams(("arbitrary", "arbitrary")),
        name="attention",
    )(q, k, vt)


def _ssm_kernel(bsz, u0_ref, u1_ref, wb_ref, wc_ref, are_ref, aim_ref, j_ref, y0_ref, y1_ref,
                bua_ref, bub_ref, st_ref, yr_ref, xsa_ref, xsb_ref, ym_ref):
    i = pl.program_id(0)

    @pl.when(i == 0)
    def _():
        st_ref[...] = jnp.zeros_like(st_ref)

        xsa_ref[...] = jnp.zeros_like(xsa_ref)
        xsb_ref[...] = jnp.zeros_like(xsb_ref)

    sub = SSM_SUB
    n_blk = SSM_W // LANES
    jrev = j_ref[...]
    u0 = u0_ref[...].astype(BF16).astype(F32)
    u1 = u1_ref[...].astype(BF16)
    u1 = jnp.stack([jnp.dot(jrev, u1[b], preferred_element_type=F32) for b in range(bsz)])
    u_dir = (u0, u1)
    row_id = lax.broadcasted_iota(jnp.int32, (N_SEQ * sub, LANES), 0)
    is_dir1 = (row_id % N_SEQ) >= bsz

    def stage_inputs(xs_ref, s):
        for d in range(2):
            for b in range(bsz):
                rows = pl.ds(d * bsz + b, sub, stride=N_SEQ)
                for kb in range(n_blk):
                    xs_ref[kb, d, rows, :] = u_dir[d][b, s * sub:(s + 1) * sub, kb * LANES:(kb + 1) * LANES]

    def input_units(xs_ref, buf_ref):
        def unit(j):
            kb = j // 4
            x = jnp.concatenate([xs_ref[kb, 0], xs_ref[kb, 1]], axis=-1).astype(BF16)
            r = jnp.dot(x, wb_ref[j], preferred_element_type=F32)
            buf_ref[2 * j] = r[:, 0:LANES]
            buf_ref[2 * j + 1] = r[:, LANES:2 * LANES]
        return [functools.partial(unit, j) for j in range(N_PAIRS)]

    def readout_units(buf_ref, s):
        def unit(m):
            acc = None
            for j in range(4 * m, 4 * m + 4):
                st = jnp.concatenate([buf_ref[2 * j], buf_ref[2 * j + 1]], axis=-1).astype(BF16)
                r = jnp.dot(st, wc_ref[j], preferred_element_type=F32)
                t = jnp.where(is_dir1, r[:, LANES:2 * LANES], r[:, 0:LANES])
                acc = t if acc is None else acc + t
            ym_ref[m] = acc
            for d in range(2):
                dst = y0_ref if d == 0 else yr_ref
                for b in range(bsz):
                    dst[b, s * sub:(s + 1) * sub, m * LANES:(m + 1) * LANES] = \
                        ym_ref[m, pl.ds(d * bsz + b, sub, stride=N_SEQ), :]
        return [functools.partial(unit, m) for m in range(n_blk)]

    def scan(buf_ref, state, side_units):
        state = list(state)
        every = max(1, sub // max(1, len(side_units)))
        pending = list(side_units)
        for t in range(sub):
            for j in range(N_PAIRS):
                sre, sim = state[2 * j], state[2 * j + 1]
                are, aim = are_ref[j], aim_ref[j]
                rows = slice(t * N_SEQ, (t + 1) * N_SEQ)
                bre = buf_ref[2 * j, rows, :]
                bim = buf_ref[2 * j + 1, rows, :]
                nre = are * sre - aim * sim + bre
                nim = are * sim + aim * sre + bim
                buf_ref[2 * j, rows, :] = nre
                buf_ref[2 * j + 1, rows, :] = nim
                state[2 * j], state[2 * j + 1] = nre, nim
            if pending and t % every == 0:
                pending.pop(0)()
        for unit in pending:
            unit()
        return state

    state = [st_ref[k] for k in range(2 * N_PAIRS)]
    stage_inputs(xsa_ref, 0)
    stage_inputs(xsb_ref, 1)
    for unit in input_units(xsa_ref, bua_ref):
        unit()
    state = scan(bua_ref, state, input_units(xsb_ref, bub_ref))
    state = scan(bub_ref, state, readout_units(bua_ref, 0))
    for k in range(2 * N_PAIRS):
        st_ref[k] = state[k]
    for unit in readout_units(bub_ref, 1):
        unit()
    for b in range(bsz):
        y = yr_ref[b]
        hi = y.astype(BF16)
        lo = (y - hi.astype(F32)).astype(BF16)
        y1_ref[b] = (jnp.dot(jrev, hi, preferred_element_type=F32)
                     + jnp.dot(jrev, lo, preferred_element_type=F32))


def _ssm(u_all, ssm):
    bsz, n_all, _ = u_all.shape
    assert N_SEQ == 2 * bsz
    nc = n_all // SSM_CHUNK
    ctx_chunks = 256 // SSM_CHUNK

    def rev_map(i):
        return jnp.where(i < ctx_chunks, ctx_chunks - 1 - i, nc - 1 - (i - ctx_chunks))

    blk = (bsz, SSM_CHUNK, SSM_W)
    return pl.pallas_call(
        functools.partial(_ssm_kernel, bsz),
        grid=(nc,),
        in_specs=[pl.BlockSpec(blk, lambda i: (0, i, 0)),
                  pl.BlockSpec(blk, lambda i: (0, rev_map(i), 0)),
                  _full(ssm["wb"].shape), _full(ssm["wc"].shape),
                  _full(ssm["a_re"].shape), _full(ssm["a_im"].shape), _full(ssm["jrev"].shape)],
        out_specs=[pl.BlockSpec(blk, lambda i: (0, i, 0)),
                   pl.BlockSpec(blk, lambda i: (0, rev_map(i), 0))],
        out_shape=[jax.ShapeDtypeStruct(u_all.shape, F32)] * 2,
        scratch_shapes=[pltpu.VMEM((2 * N_PAIRS, N_SEQ * SSM_SUB, LANES), F32),
                        pltpu.VMEM((2 * N_PAIRS, N_SEQ * SSM_SUB, LANES), F32),
                        pltpu.VMEM((2 * N_PAIRS, N_SEQ, LANES), F32),
                        pltpu.VMEM(blk, F32),
                        pltpu.VMEM((SSM_W // LANES, 2, N_SEQ * SSM_SUB, LANES), F32),
                        pltpu.VMEM((SSM_W // LANES, 2, N_SEQ * SSM_SUB, LANES), F32),
                        pltpu.VMEM((SSM_W // LANES, N_SEQ * SSM_SUB, LANES), F32)],
        compiler_params=_cparams(("arbitrary",)),
        name="ssm",
    )(u_all, u_all, ssm["wb"], ssm["wc"], ssm["a_re"], ssm["a_im"], ssm["jrev"])


def _layer_norm(v, g, b):
    mu = jnp.mean(v, -1, keepdims=True)
    var = jnp.mean(jnp.square(v - mu), -1, keepdims=True)
    return (v - mu) * lax.rsqrt(var + EPS) * g + b


def _gelu_tanh(v):
    return 0.5 * v * (1.0 + jnp.tanh(math.sqrt(2.0 / math.pi) * (v + 0.044715 * (v * v * v))))


def _route(logits):
    lane = lax.broadcasted_iota(jnp.int32, logits.shape, 1).astype(F32)
    neg = jnp.float32(-jnp.inf)
    big = jnp.float32(1 << 20)
    gl = jnp.where((lane >= N_EXPERTS) & (lane < N_EXPERTS + N_EGROUPS), logits, neg)
    gmax = jnp.max(gl, -1, keepdims=True)
    g_w = 1.0 / jnp.sum(jnp.exp(gl - gmax), -1, keepdims=True)
    g_lane = jnp.min(jnp.where(gl == gmax, lane, big), -1, keepdims=True)
    lo = (g_lane - N_EXPERTS) * E_PER_GROUP
    el = jnp.where((lane >= lo) & (lane < lo + E_PER_GROUP), logits, neg)
    m1 = jnp.max(el, -1, keepdims=True)
    i1 = jnp.min(jnp.where(el == m1, lane, big), -1, keepdims=True)
    el2 = jnp.where(lane == i1, neg, el)
    m2 = jnp.max(el2, -1, keepdims=True)
    i2 = jnp.min(jnp.where(el2 == m2, lane, big), -1, keepdims=True)
    esum = jnp.sum(jnp.exp(el - m1), -1, keepdims=True)
    p1 = 1.0 / esum
    p2 = jnp.exp(m2 - m1) / esum
    w1 = p1 / (p1 + p2) * g_w
    w2 = p2 / (p1 + p2) * g_w
    onehot = jnp.where((lane == i1) | (lane == i2 + N_EXPERTS), 1.0, 0.0)
    info = jnp.where(lane == 0, i1, jnp.where(lane == 1, i2, jnp.where(lane == 2, w1, jnp.where(lane == 3, w2, 0.0))))
    return onehot, info


def _merge_kernel(at_ref, y0_ref, y1_ref, u_ref, x_ref, mod_ref, dsk_ref, wglu_ref, bglu_ref, gna_ref, gns_ref,
                  wo_ref, ln1g_ref, ln1b_ref, wrh_ref, wrl_ref, br_ref, x1_ref, hp_ref, oh_ref, info_ref, hs_ref):
    b = pl.program_id(0)
    mod = mod_ref[pl.ds(b, 1), :]
    gate_a = mod[:, 2 * D_MODEL:3 * D_MODEL]
    shift_f = mod[:, 3 * D_MODEL:4 * D_MODEL]
    scale_f = mod[:, 4 * D_MODEL:5 * D_MODEL]

    g = _gelu_tanh(y0_ref[0] + y1_ref[0] + dsk_ref[...] * u_ref[0])
    z = jnp.dot(g.astype(BF16), wglu_ref[...], preferred_element_type=F32) + bglu_ref[...]
    ssm = g * jax.nn.sigmoid(z)
    attn = at_ref[0].T
    an = _rms(attn, gna_ref[...]).astype(BF16)
    sn = _rms(ssm, gns_ref[...]).astype(BF16)
    mix = (jnp.dot(an, wo_ref[0:ATTN_W, :], preferred_element_type=F32)
           + jnp.dot(sn, wo_ref[ATTN_W:ATTN_W + SSM_W, :], preferred_element_type=F32))
    x1 = _layer_norm(ALPHA * x_ref[0] + gate_a * mix, ln1g_ref[...], ln1b_ref[...])
    x1_ref[0] = x1
    h = x1 * (1.0 + scale_f) + shift_f
    h_hi = h.astype(BF16)
    h_hif = h_hi.astype(F32)
    h_lo = (h - h_hif).astype(BF16)
    logits = (jnp.dot(h_hi, wrh_ref[...], preferred_element_type=F32)
              + jnp.dot(h_lo, wrh_ref[...], preferred_element_type=F32)
              + jnp.dot(h_hi, wrl_ref[...], preferred_element_type=F32)) + br_ref[...]
    onehot, info = _route(logits)
    oh_ref[...] = onehot.astype(BF16)
    info_ref[...] = info
    t = h.shape[0]
    for c in range(SLAB_ROWS):
        hs_ref[pl.ds(c, t, stride=SLAB_ROWS), :] = h[:, c * LANES:(c + 1) * LANES]
    hp_ref[...] = hs_ref[...].astype(BF16)


def _merge(attn_t, y0, y1, u_all, x, mod, prep, ctx_len):
    bsz, n, _ = x.shape
    off = ctx_len // ROW_TILE
    tok = lambda w: pl.BlockSpec((1, ROW_TILE, w), lambda b, t: (b, t, 0))
    tok_off = lambda w: pl.BlockSpec((1, ROW_TILE, w), lambda b, t: (b, t + off, 0))
    weights = [prep["d_skip"], prep["w_glu"], prep["b_glu"], prep["gn_attn"], prep["gn_ssm"], prep["w_o"],
               prep["ln1_g"], prep["ln1_b"], prep["w_router_hi"], prep["w_router_lo"], prep["b_router"]]
    nt = n // ROW_TILE
    flat = lambda rows: pl.BlockSpec((rows, LANES), lambda b, t: (b * nt + t, 0))
    return pl.pallas_call(
        _merge_kernel,
        grid=(bsz, nt),
        in_specs=[pl.BlockSpec((1, ATTN_W, ROW_TILE), lambda b, t: (b, 0, t)),
                  tok_off(SSM_W), tok_off(SSM_W), tok_off(SSM_W), tok(D_MODEL), _full(mod.shape)]
                 + [_full(w.shape) for w in weights],
        out_specs=[tok(D_MODEL), flat(ROW_TILE * SLAB_ROWS), flat(ROW_TILE), flat(ROW_TILE)],
        out_shape=[jax.ShapeDtypeStruct((bsz, n, D_MODEL), F32),
                   jax.ShapeDtypeStruct((bsz * n * SLAB_ROWS, LANES), BF16),
                   jax.ShapeDtypeStruct((bsz * n, LANES), BF16),
                   jax.ShapeDtypeStruct((bsz * n, LANES), F32)],
        scratch_shapes=[pltpu.VMEM((ROW_TILE * SLAB_ROWS, LANES), F32)],
        compiler_params=_cparams(("arbitrary", "arbitrary")),
        name="merge",
    )(attn_t, y0, y1, u_all, x, mod, *weights)


def _rank_kernel(oh_ref, tri_ref, rank_ref, cnt_ref, carry_ref):
    @pl.when(pl.program_id(0) == 0)
    def _():
        carry_ref[...] = jnp.zeros_like(carry_ref)

    oh = oh_ref[...]
    ohf = oh.astype(F32)
    before = jnp.dot(tri_ref[...], oh, preferred_element_type=F32) + carry_ref[0:1, :]
    lane = lax.broadcasted_iota(jnp.int32, ohf.shape, 1)
    hit = ohf * before
    r1 = jnp.sum(jnp.where(lane < N_EXPERTS, hit, 0.0), -1, keepdims=True)
    r2 = jnp.sum(jnp.where(lane >= N_EXPERTS, hit, 0.0), -1, keepdims=True)
    rank_ref[...] = jnp.where(lane == 0, r1, jnp.where(lane == 1, r2, 0.0))
    total = carry_ref[...] + jnp.sum(ohf, 0, keepdims=True)
    carry_ref[...] = total
    cnt_ref[...] = total


def _rank(onehot):
    rows = onehot.shape[0]
    tile = min(RANK_TILE, rows)
    tri = jnp.tril(jnp.ones((tile, tile), F32), -1).astype(BF16)
    return pl.pallas_call(
        _rank_kernel,
        grid=(rows // tile,),
        in_specs=[pl.BlockSpec((tile, LANES), lambda i: (i, 0)), _full(tri.shape)],
        out_specs=[pl.BlockSpec((tile, LANES), lambda i: (i, 0)), _full((8, LANES))],
        out_shape=[jax.ShapeDtypeStruct((rows, LANES), F32), jax.ShapeDtypeStruct((8, LANES), F32)],
        scratch_shapes=[pltpu.VMEM((8, LANES), F32)],
        compiler_params=_cparams(("arbitrary",)),
        name="rank",
    )(onehot, tri)


def _expert_kernel(n_tok, te_ref, tn_ref, src_ref, hp_hbm, wg_ref, wu_ref, wd_ref, y_hbm,
                   hres_ref, xbuf_ref, yslab_ref, pend_ref, load_sem, out_sem):
    i = pl.program_id(0)
    last = pl.num_programs(0) - 1
    slot = i % 2
    tm = EXPERT_TILE
    n_valid = tn_ref[i]

    def wait_rows(s):
        rows = pend_ref[s] * SLAB_ROWS
        pltpu.make_async_copy(yslab_ref.at[s, pl.ds(0, rows), :], y_hbm.at[pl.ds(0, rows), :], out_sem.at[s]).wait()
        pend_ref[s] = 0

    def row_copy(r, a):
        return pltpu.make_async_copy(yslab_ref.at[slot, pl.ds(pl.multiple_of(r * SLAB_ROWS, SLAB_ROWS), SLAB_ROWS), :],
                                     y_hbm.at[pl.ds(pl.multiple_of(a * SLAB_ROWS, SLAB_ROWS), SLAB_ROWS), :],
                                     out_sem.at[slot])

    @pl.when(i == 0)
    def _():
        pend_ref[0] = 0
        pend_ref[1] = 0
        cp = pltpu.make_async_copy(hp_hbm, hres_ref, load_sem)
        cp.start()
        cp.wait()

    @pl.when(n_valid > 0)
    def _():
        @pl.when(pend_ref[slot] > 0)
        def _():
            wait_rows(slot)

        for r in range(tm):
            tok = src_ref[0, 0, r] & (n_tok - 1)
            pair = hres_ref[pl.ds(pl.multiple_of((tok >> 1) * (2 * SLAB_ROWS), 2 * SLAB_ROWS), 2 * SLAB_ROWS), :]
            pair = pair.astype(F32)
            xbuf_ref[pl.ds(r, SLAB_ROWS, stride=XBUF_PITCH), :] = jnp.where(
                (tok & 1) == 1, pair[SLAB_ROWS:2 * SLAB_ROWS], pair[0:SLAB_ROWS])
        x = jnp.concatenate([xbuf_ref[c * XBUF_PITCH:c * XBUF_PITCH + tm, :] for c in range(SLAB_ROWS)],
                            axis=-1).astype(BF16)
        a = jnp.dot(x, wg_ref[0].astype(BF16), preferred_element_type=F32)
        u = jnp.dot(x, wu_ref[0].astype(BF16), preferred_element_type=F32)
        act = (a * jax.nn.sigmoid(a) * u).astype(BF16)
        y = jnp.dot(act, wd_ref[0].astype(BF16), preferred_element_type=F32)
        for c in range(SLAB_ROWS):
            yslab_ref[slot, pl.ds(c, tm, stride=SLAB_ROWS), :] = y[:, c * LANES:(c + 1) * LANES]

        @pl.when(n_valid == tm)
        def _():
            for r in range(tm):
                row_copy(r, src_ref[0, 0, r]).start()

        @pl.when(n_valid < tm)
        def _():
            def issue(r, carry):
                row_copy(r, src_ref[0, 0, r]).start()
                return carry
            lax.fori_loop(0, n_valid, issue, 0)

        pend_ref[slot] = n_valid

    @pl.when(i == last)
    def _():
        for s in range(2):
            @pl.when(pend_ref[s] > 0)
            def _():
                wait_rows(s)


def _experts(tile_e, tile_n, src, hp, wg, wu, wd, n_tok):
    n_tiles = tile_e.shape[0]
    tm = EXPERT_TILE
    assert n_tok & (n_tok - 1) == 0
    wspec = lambda shape: pl.BlockSpec((1,) + shape, lambda i, te, tn: (te[i], 0, 0))
    grid_spec = pltpu.PrefetchScalarGridSpec(
        num_scalar_prefetch=2,
        grid=(n_tiles,),
        in_specs=[pl.BlockSpec((1, 1, tm), lambda i, te, tn: (i, 0, 0), memory_space=pltpu.SMEM),
                  pl.BlockSpec(memory_space=pl.ANY),
                  wspec((D_MODEL, D_FF)), wspec((D_MODEL, D_FF)), wspec((D_FF, D_MODEL))],
        out_specs=pl.BlockSpec(memory_space=pl.ANY),
        scratch_shapes=[pltpu.VMEM(hp.shape, BF16),
                        pltpu.VMEM((SLAB_ROWS * XBUF_PITCH, LANES), F32),
                        pltpu.VMEM((2, tm * SLAB_ROWS, LANES), F32),
                        pltpu.SMEM((2,), jnp.int32),
                        pltpu.SemaphoreType.DMA(()),
                        pltpu.SemaphoreType.DMA((2,))])
    return pl.pallas_call(
        functools.partial(_expert_kernel, n_tok),
        grid_spec=grid_spec,
        out_shape=jax.ShapeDtypeStruct((2 * n_tok * SLAB_ROWS, LANES), F32),
        compiler_params=pltpu.CompilerParams(dimension_semantics=("arbitrary",),
                                             vmem_limit_bytes=EXPERT_VMEM_LIMIT),
        name="experts",
    )(tile_e, tile_n, src.reshape(n_tiles, 1, tm), hp, wg, wu, wd)


def _combine_kernel(y0_ref, y1_ref, info_ref, x1_ref, mod_ref, ln2g_ref, ln2b_ref, o_ref):
    b = pl.program_id(0)
    gate_f = mod_ref[pl.ds(b, 1), 5 * D_MODEL:6 * D_MODEL]
    t = x1_ref.shape[1]
    info = info_ref[...]
    w1, w2 = info[:, 2:3], info[:, 3:4]
    ffn = jnp.concatenate([w1 * y0_ref[pl.ds(c, t, stride=SLAB_ROWS), :] + w2 * y1_ref[pl.ds(c, t, stride=SLAB_ROWS), :]
                           for c in range(SLAB_ROWS)], axis=-1)
    o_ref[0] = _layer_norm(ALPHA * x1_ref[0] + gate_f * ffn, ln2g_ref[...], ln2b_ref[...])


def _combine(y_slots, info, x1, mod, ln2_g, ln2_b):
    bsz, n, _ = x1.shape
    nt = n // ROW_TILE
    slab = lambda k: pl.BlockSpec((ROW_TILE * SLAB_ROWS, LANES), lambda b, t: (k * bsz * nt + b * nt + t, 0))
    return pl.pallas_call(
        _combine_kernel,
        grid=(bsz, nt),
        in_specs=[slab(0), slab(1), pl.BlockSpec((ROW_TILE, LANES), lambda b, t: (b * nt + t, 0)),
                  pl.BlockSpec((1, ROW_TILE, D_MODEL), lambda b, t: (b, t, 0)),
                  _full(mod.shape), _full(ln2_g.shape), _full(ln2_b.shape)],
        out_specs=pl.BlockSpec((1, ROW_TILE, D_MODEL), lambda b, t: (b, t, 0)),
        out_shape=jax.ShapeDtypeStruct((bsz, n, D_MODEL), F32),
        compiler_params=_cparams(("arbitrary", "arbitrary")),
        name="combine",
    )(y_slots, y_slots, info, x1, mod, ln2_g, ln2_b)


def _invert_kernel(n_tok, n_tiles, cnt_ref, e1_ref, e2_ref, r1_ref, r2_ref, src_ref, te_ref, tn_ref, base_ref):
    step = pl.program_id(0)
    tm = EXPERT_TILE
    chunk = e1_ref.shape[2]

    @pl.when(step == 0)
    def _():
        def fill(p, c):
            src_ref[p] = 0
            return c

        def per_expert(e, carry):
            tile0, last_e = carry
            c1 = cnt_ref[e]
            cnt = c1 + cnt_ref[N_EXPERTS + e]
            nt = (cnt + (tm - 1)) // tm
            base_ref[e] = tile0 * tm
            base_ref[N_EXPERTS + e] = tile0 * tm + c1

            def per_tile(j, c):
                te_ref[tile0 + j] = e
                tn_ref[tile0 + j] = jnp.minimum(cnt - j * tm, tm)
                return c
            lax.fori_loop(0, nt, per_tile, 0)
            lax.fori_loop(tile0 * tm + cnt, (tile0 + nt) * tm, fill, 0)
            return tile0 + nt, jnp.where(nt > 0, e, last_e)

        used, last_e = lax.fori_loop(0, N_EXPERTS, per_expert, (jnp.int32(0), jnp.int32(0)))

        def idle(j, c):
            te_ref[j] = last_e
            tn_ref[j] = 0
            return c
        lax.fori_loop(used, n_tiles, idle, 0)
        lax.fori_loop(used * tm, n_tiles * tm, fill, 0)

    def body(j, c):
        t = step * chunk + j
        src_ref[base_ref[e1_ref[0, 0, j]] + r1_ref[0, 0, j]] = t
        src_ref[base_ref[N_EXPERTS + e2_ref[0, 0, j]] + r2_ref[0, 0, j]] = n_tok + t
        return c
    lax.fori_loop(0, chunk, body, 0, unroll=8)


def _invert(info, rank, counts, n_tok):
    tm = EXPERT_TILE
    n_tiles = 2 * n_tok // tm + N_EXPERTS
    chunk = min(INVERT_CHUNK, n_tok)
    nc = n_tok // chunk
    col = lambda a, j: a[:, j].astype(jnp.int32).reshape(nc, 1, chunk)
    cnt = counts[0, 0:2 * N_EXPERTS].astype(jnp.int32)
    idx_spec = pl.BlockSpec((1, 1, chunk), lambda i, c: (i, 0, 0), memory_space=pltpu.SMEM)
    smem_out = pl.BlockSpec(memory_space=pltpu.SMEM)
    grid_spec = pltpu.PrefetchScalarGridSpec(
        num_scalar_prefetch=1, grid=(nc,),
        in_specs=[idx_spec] * 4,
        out_specs=[smem_out] * 3,
        scratch_shapes=[pltpu.SMEM((2 * N_EXPERTS,), jnp.int32)])
    return pl.pallas_call(
        functools.partial(_invert_kernel, n_tok, n_tiles),
        grid_spec=grid_spec,
        out_shape=[jax.ShapeDtypeStruct((n_tiles * tm,), jnp.int32),
                   jax.ShapeDtypeStruct((n_tiles,), jnp.int32),
                   jax.ShapeDtypeStruct((n_tiles,), jnp.int32)],
        compiler_params=_cparams(("arbitrary",)),
        name="invert",
    )(cnt, col(info, 0), col(info, 1), col(rank, 0), col(rank, 1))


def _rope_tables(n):
    rows = n // GRID_W
    row = jnp.repeat(jnp.arange(rows), GRID_W).astype(F32)
    col = jnp.tile(jnp.arange(GRID_W), rows).astype(F32)
    axis_dim = QK_ROPE // 2
    inv_freq = ROPE_BASE ** (-jnp.arange(0, axis_dim, 2, dtype=F32) / axis_dim)
    ang = jnp.concatenate([row[:, None] * inv_freq, col[:, None] * inv_freq], -1)
    ang = jnp.concatenate([ang, ang], -1)
    pad_l = jnp.ones((n, QK_NOPE), F32)
    pad_r = jnp.ones((n, HEAD_PAD - QK_NOPE - QK_ROPE), F32)
    cos_t = jnp.concatenate([pad_l, jnp.cos(ang), pad_r], -1)
    sin_t = jnp.concatenate([0 * pad_l, jnp.sin(ang), 0 * pad_r], -1)
    return cos_t, sin_t


def _rot_half_cols(w):
    w1, w2 = w[..., :QK_ROPE // 2], w[..., QK_ROPE // 2:]
    return jnp.concatenate([-w2, w1], -1)


def _prep_project(w_in, q_norm_g, kv_norm_g, w_uq, w_ukv):
    zeros = lambda r, c: jnp.zeros((r, c), F32)
    w_in_p = jnp.concatenate([w_in[:, :MLA_IN], zeros(D_MODEL, 512 - MLA_IN), w_in[:, MLA_IN:]], -1)
    qk_scale = SM_SCALE * math.log2(math.e)
    wq = (w_uq * qk_scale).reshape(Q_RANK, N_HEADS, QK_NOPE + QK_ROPE)
    pad = jnp.zeros((Q_RANK, N_HEADS, HEAD_PAD - QK_NOPE - QK_ROPE), F32)
    wq_pad = jnp.concatenate([wq, pad], -1).reshape(Q_RANK, N_HEADS * HEAD_PAD)
    wq_rot = jnp.concatenate([0 * wq[..., :QK_NOPE], _rot_half_cols(wq[..., QK_NOPE:]), pad], -1)
    wq_rot = wq_rot.reshape(Q_RANK, N_HEADS * HEAD_PAD)
    wkv = w_ukv.reshape(KV_RANK, N_HEADS, QK_NOPE + V_DIM)
    wk = jnp.concatenate([wkv[..., :QK_NOPE], jnp.zeros((KV_RANK, N_HEADS, HEAD_PAD - QK_NOPE), F32)], -1)
    wk = wk.reshape(KV_RANK, N_HEADS * HEAD_PAD)
    wvt = wkv[..., QK_NOPE:].reshape(KV_RANK, ATTN_W).T
    eye = jnp.eye(QK_ROPE, dtype=F32)
    place = lambda m: jnp.zeros((LANES, HEAD_PAD), F32).at[:QK_ROPE, QK_NOPE:QK_NOPE + QK_ROPE].set(m)
    return dict(w_in=w_in_p.astype(BF16), q_g=q_norm_g.reshape(1, -1), kv_g=kv_norm_g.reshape(1, -1),
                wq=wq_pad.astype(BF16), wq_rot=wq_rot.astype(BF16), wk=wk.astype(BF16),
                wvt=wvt.astype(BF16), e=place(eye).astype(BF16), e_rot=place(_rot_half_cols(eye)).astype(BF16))


def _discretise(a_re, a_im, log_dt, b_re, b_im):
    dt = jnp.exp(log_dt)[:, None]
    mag = jnp.exp(a_re * dt)
    abar_re, abar_im = mag * jnp.cos(a_im * dt), mag * jnp.sin(a_im * dt)
    den = jnp.square(a_re) + jnp.square(a_im)
    num_re = abar_re - 1.0
    coef_re = (num_re * a_re + abar_im * a_im) / den
    coef_im = (abar_im * a_re - num_re * a_im) / den
    bbar_re = coef_re[..., None] * b_re - coef_im[..., None] * b_im
    bbar_im = coef_re[..., None] * b_im + coef_im[..., None] * b_re
    return abar_re, abar_im, bbar_re, bbar_im


def _prep_ssm(a_re, a_im, log_dt, b_re, b_im, c_re, c_im, bsz):
    wbs, wcs, ares, aims = [], [], [], []
    eye2 = jnp.eye(2, dtype=F32)
    quarter = jax.nn.one_hot(jnp.arange(N_PAIRS) % 4, 4, dtype=F32)
    for d in range(2):
        abar_re, abar_im, bbar_re, bbar_im = _discretise(a_re[d], a_im[d], log_dt[d], b_re[d], b_im[d])

        def in_block(bb):
            t = bb.reshape(N_PAIRS, 2, SSM_STATE, SSM_GROUP)
            return jnp.einsum('jgph,gk->jghkp', t, eye2).reshape(N_PAIRS, 2 * SSM_GROUP, 2 * SSM_STATE)

        blk = jnp.concatenate([in_block(bbar_re), in_block(bbar_im)], -1)
        wbs.append(jnp.einsum('jrc,jq->jqrc', blk, quarter).reshape(N_PAIRS, LANES, 2 * LANES))

        def out_block(cc):
            t = cc.reshape(N_PAIRS, 2, SSM_GROUP, SSM_STATE)
            return jnp.einsum('jghp,gk->jgpkh', t, eye2).reshape(N_PAIRS, 2 * SSM_STATE, 2 * SSM_GROUP)

        oblk = jnp.concatenate([out_block(c_re[d]), -out_block(c_im[d])], 1)
        wcs.append(jnp.einsum('jkc,jq->jkqc', oblk, quarter).reshape(N_PAIRS, 2 * LANES, LANES))
        ares.append(jnp.broadcast_to(abar_re.reshape(N_PAIRS, 1, LANES), (N_PAIRS, bsz, LANES)))
        aims.append(jnp.broadcast_to(abar_im.reshape(N_PAIRS, 1, LANES), (N_PAIRS, bsz, LANES)))
    jrev = jnp.eye(SSM_CHUNK, dtype=F32)[::-1]
    return dict(wb=jnp.concatenate(wbs, axis=1).astype(BF16), wc=jnp.concatenate(wcs, axis=2).astype(BF16),
                a_re=jnp.concatenate(ares, 1), a_im=jnp.concatenate(aims, 1), jrev=jrev.astype(BF16))


def kernel(x, c, ctx, c_ctx, w_ada, b_ada, w_in, q_norm_g, kv_norm_g, w_uq, w_ukv, ssm_a_re, ssm_a_im,
           ssm_log_dt, ssm_b_re, ssm_b_im, ssm_c_re, ssm_c_im, ssm_d, w_glu, b_glu, gn_attn_g, gn_ssm_g, w_o,
           ln1_g, ln1_b, w_router_group, b_router_group, w_router_expert, b_router_expert, w_exp_gate,
           w_exp_up, w_exp_down, ln2_g, ln2_b):
    bsz, n, _ = x.shape
    ctx_len = ctx.shape[1]
    n_all = ctx_len + n
    l = 0
    row = lambda v: v.reshape(1, -1)

    cvec = jnp.concatenate([c, c_ctx[None, :], jnp.zeros((8 - bsz - 1, D_MODEL), F32)], 0)
    mod = _adaln(cvec, w_ada[l], b_ada[l])

    prep = _prep_project(w_in[l], q_norm_g[l], kv_norm_g[l], w_uq[l], w_ukv[l])
    cos_t, sin_t = _rope_tables(n)
    ones_t = jnp.ones((ctx_len, HEAD_PAD), F32)
    q, k, vt, u_all = _project(x, mod, cos_t, sin_t, prep, n_all, ctx_len)
    k, vt, u_all = _project(ctx, mod, ones_t, 0 * ones_t, prep, n_all, 0, prev=(k, vt, u_all))

    attn_t = _attention(q, k, vt)

    ssm = _prep_ssm(ssm_a_re[l], ssm_a_im[l], ssm_log_dt[l], ssm_b_re[l], ssm_b_im[l],
                    ssm_c_re[l], ssm_c_im[l], bsz)
    y0, y1 = _ssm(u_all, ssm)

    w_router = jnp.concatenate([w_router_expert[l], w_router_group[l],
                                jnp.zeros((D_MODEL, LANES - N_EXPERTS - N_EGROUPS), F32)], -1)
    b_router = jnp.concatenate([b_router_expert[l], b_router_group[l],
                                jnp.zeros((LANES - N_EXPERTS - N_EGROUPS,), F32)])
    w_router_hi = w_router.astype(BF16)
    mprep = dict(d_skip=row(ssm_d[l]), w_glu=w_glu[l].astype(BF16), b_glu=row(b_glu[l]),
                 gn_attn=row(gn_attn_g[l]), gn_ssm=row(gn_ssm_g[l]), w_o=w_o[l].astype(BF16),
                 ln1_g=row(ln1_g[l]), ln1_b=row(ln1_b[l]), w_router_hi=w_router_hi,
                 w_router_lo=(w_router - w_router_hi.astype(F32)).astype(BF16), b_router=row(b_router))
    x1, hp, onehot, info = _merge(attn_t, y0, y1, u_all, x, mod, mprep, ctx_len)

    rank, counts = _rank(onehot)
    src, tile_e, tile_n = _invert(info, rank, counts, bsz * n)
    y_slots = _experts(tile_e, tile_n, src, hp, w_exp_gate[l], w_exp_up[l], w_exp_down[l], bsz * n)
    return _combine(y_slots, info, x1, mod, row(ln2_g[l]), row(ln2_b[l]))
```

```python
import functools
import math

import jax
import jax.numpy as jnp
from jax import lax
from jax.experimental import pallas as pl
from jax.experimental.pallas import tpu as pltpu

F32 = jnp.float32
BF16 = jnp.bfloat16

D_MODEL = 1024
GRID_W = 64
N_HEADS = 8
QK_NOPE = 64
QK_ROPE = 32
V_DIM = 64
Q_RANK = 256
KV_RANK = 128
ATTN_W = N_HEADS * V_DIM
MLA_IN = Q_RANK + KV_RANK + QK_ROPE
ROPE_BASE = 10000.0
SM_SCALE = (QK_NOPE + QK_ROPE) ** -0.5
SSM_W = D_MODEL - ATTN_W
SSM_GROUP = 16
N_GROUPS = SSM_W // SSM_GROUP
SSM_STATE = 64
N_EGROUPS = 4
E_PER_GROUP = 8
N_EXPERTS = N_EGROUPS * E_PER_GROUP
D_FF = 256
DEPTH = 1
ALPHA = (2 * DEPTH) ** 0.25
EPS = 1e-6

LANES = 128
HEAD_PAD = LANES
VMEM_LIMIT = 48 * 1024 * 1024

ROW_TILE = 256
Q_TILE = 256
KV_CHUNKS = 2
SSM_CHUNK = 128
SSM_SUB = SSM_CHUNK // 2
N_SEQ = 8
N_PAIRS = N_GROUPS // 2
RANK_TILE = 512
EXPERT_TILE_LOG2 = 8
EXPERT_TILE = 1 << EXPERT_TILE_LOG2
INVERT_CHUNK = 2048
SLAB_ROWS = D_MODEL // LANES
XBUF_PITCH = EXPERT_TILE + 8
EXPERT_VMEM_LIMIT = 56 * 1024 * 1024


def _cparams(sem):
    return pltpu.CompilerParams(dimension_semantics=sem, vmem_limit_bytes=VMEM_LIMIT)


def _full(shape):
    return pl.BlockSpec(shape, lambda *_: (0,) * len(shape))


def _adaln_kernel(c_ref, w_ref, b_ref, o_ref):
    c = c_ref[...]
    s = c * jax.nn.sigmoid(c)
    o_ref[...] = jnp.dot(s, w_ref[...], preferred_element_type=F32,
                         precision=lax.Precision.HIGHEST) + b_ref[...]


def _adaln(cvec, w_ada, b_ada):
    n_out = w_ada.shape[1]
    tn = 1536
    return pl.pallas_call(
        _adaln_kernel,
        grid=(n_out // tn,),
        in_specs=[_full(cvec.shape),
                  pl.BlockSpec((D_MODEL, tn), lambda j: (0, j)),
                  pl.BlockSpec((1, tn), lambda j: (0, j))],
        out_specs=pl.BlockSpec((cvec.shape[0], tn), lambda j: (0, j)),
        out_shape=jax.ShapeDtypeStruct((cvec.shape[0], n_out), F32),
        compiler_params=_cparams(("arbitrary",)),
        name="adaln",
    )(cvec, w_ada, b_ada.reshape(1, n_out))


def _rms(v, g):
    return v * lax.rsqrt(jnp.mean(jnp.square(v), -1, keepdims=True) + EPS) * g


def _project_kernel(with_q, mod_row_fn, x_ref, mod_ref, cos_ref, sin_ref, w_in_ref, qg_ref, kvg_ref,
                    wq_ref, wqr_ref, wk_ref, wvt_ref, e_ref, er_ref, *rest):
    if with_q:
        q_ref, k_ref, vt_ref, u_ref = rest
    else:
        k_ref, vt_ref, u_ref = rest[-3:]
    row = mod_row_fn(pl.program_id(0))
    mod = mod_ref[pl.ds(row, 1), :]
    shift = mod[:, 0:D_MODEL]
    scale = mod[:, D_MODEL:2 * D_MODEL]
    xm = (x_ref[0] * (1.0 + scale) + shift).astype(BF16)
    p = jnp.dot(xm, w_in_ref[...], preferred_element_type=F32)
    u_ref[0] = p[:, 512:1024]
    cos = cos_ref[...]
    sin = sin_ref[...]

    ckvn = _rms(p[:, Q_RANK:Q_RANK + KV_RANK], kvg_ref[...]).astype(BF16)
    kn = jnp.dot(ckvn, wk_ref[...], preferred_element_type=F32)
    kr = p[:, 384:512].astype(BF16)
    krope = (jnp.dot(kr, e_ref[...], preferred_element_type=F32) * cos
             + jnp.dot(kr, er_ref[...], preferred_element_type=F32) * sin)
    for h in range(N_HEADS):
        k_ref[0, h] = (kn[:, h * HEAD_PAD:(h + 1) * HEAD_PAD] + krope).astype(BF16)
    vt_ref[0] = lax.dot_general(wvt_ref[...], ckvn, (((1,), (1,)), ((), ())),
                                preferred_element_type=F32).astype(BF16)

    if with_q:
        qn = _rms(p[:, 0:Q_RANK], qg_ref[...]).astype(BF16)
        qa = jnp.dot(qn, wq_ref[...], preferred_element_type=F32)
        qb = jnp.dot(qn, wqr_ref[...], preferred_element_type=F32)
        for h in range(N_HEADS):
            sl = slice(h * HEAD_PAD, (h + 1) * HEAD_PAD)
            q_ref[0, h] = (qa[:, sl] * cos + qb[:, sl] * sin).astype(BF16)


def _project(x, mod, cos_t, sin_t, prep, n_all, row_off, prev=None):
    bsz, n, _ = x.shape
    with_q = prev is None
    nt = n // ROW_TILE
    off = row_off // ROW_TILE
    mod_row_fn = (lambda b: b) if with_q else (lambda b: bsz)
    w_specs = [_full(prep["w_in"].shape), _full(prep["q_g"].shape), _full(prep["kv_g"].shape),
               _full(prep["wq"].shape), _full(prep["wq_rot"].shape), _full(prep["wk"].shape),
               _full(prep["wvt"].shape), _full(prep["e"].shape), _full(prep["e_rot"].shape)]
    in_specs = [pl.BlockSpec((1, ROW_TILE, D_MODEL), lambda b, t: (b, t, 0)),
                _full(mod.shape),
                pl.BlockSpec((ROW_TILE, HEAD_PAD), lambda b, t: (t, 0)),
                pl.BlockSpec((ROW_TILE, HEAD_PAD), lambda b, t: (t, 0))] + w_specs
    k_spec = pl.BlockSpec((1, N_HEADS, ROW_TILE, HEAD_PAD), lambda b, t: (b, 0, t + off, 0))
    vt_spec = pl.BlockSpec((1, ATTN_W, ROW_TILE), lambda b, t: (b, 0, t + off))
    u_spec = pl.BlockSpec((1, ROW_TILE, SSM_W), lambda b, t: (b, t + off, 0))
    k_shape = jax.ShapeDtypeStruct((bsz, N_HEADS, n_all, HEAD_PAD), BF16)
    vt_shape = jax.ShapeDtypeStruct((bsz, ATTN_W, n_all), BF16)
    u_shape = jax.ShapeDtypeStruct((bsz, n_all, SSM_W), F32)
    args = [x, mod, cos_t, sin_t, prep["w_in"], prep["q_g"], prep["kv_g"], prep["wq"], prep["wq_rot"],
            prep["wk"], prep["wvt"], prep["e"], prep["e_rot"]]
    if with_q:
        out_specs = [pl.BlockSpec((1, N_HEADS, ROW_TILE, HEAD_PAD), lambda b, t: (b, 0, t, 0)),
                     k_spec, vt_spec, u_spec]
        out_shape = [jax.ShapeDtypeStruct((bsz, N_HEADS, n, HEAD_PAD), BF16), k_shape, vt_shape, u_shape]
        aliases = {}
    else:
        k_prev, vt_prev, u_prev = prev
        in_specs += [pl.BlockSpec(memory_space=pl.ANY)] * 3
        args += [k_prev, vt_prev, u_prev]
        out_specs = [k_spec, vt_spec, u_spec]
        out_shape = [k_shape, vt_shape, u_shape]
        aliases = {13: 0, 14: 1, 15: 2}
    return pl.pallas_call(
        functools.partial(_project_kernel, with_q, mod_row_fn), grid=(bsz, nt), in_specs=in_specs, out_specs=out_specs, out_shape=out_shape,
        input_output_aliases=aliases,
        compiler_params=_cparams(("arbitrary", "arbitrary")),
        name="project_lat" if with_q else "project_ctx",
    )(*args)


def _attn_kernel(n_chunks, q_ref, k_ref, vt_ref, o_ref, sa_ref, sb_ref):
    kv_chunk = k_ref.shape[2] // n_chunks

    def scores(h, s_ref):
        q = q_ref[0, h]
        m = None
        for c in range(n_chunks):
            kc = k_ref[0, h, pl.ds(c * kv_chunk, kv_chunk), :]
            s = lax.dot_general(kc, q, (((1,), (1,)), ((), ())), preferred_element_type=F32)
            s_ref[pl.ds(c * kv_chunk, kv_chunk), :] = s
            cm = jnp.max(s, axis=0, keepdims=True)
            m = cm if m is None else jnp.maximum(m, cm)
        return m

    def softmax_pv(h, s_ref, m):
        l = jnp.zeros_like(m)
        acc = jnp.zeros((V_DIM, m.shape[1]), F32)
        row = pl.ds(pl.multiple_of(h * V_DIM, V_DIM), V_DIM)
        for c in range(n_chunks):
            p = jnp.exp2(s_ref[pl.ds(c * kv_chunk, kv_chunk), :] - m)
            l = l + jnp.sum(p, axis=0, keepdims=True)
            vc = vt_ref[0, row, pl.ds(c * kv_chunk, kv_chunk)]
            acc = acc + jnp.dot(vc, p.astype(BF16), preferred_element_type=F32)
        o_ref[0, row, :] = acc / l

    def pair(j, m_even):
        h = 2 * j
        m_odd = scores(h + 1, sb_ref)
        softmax_pv(h, sa_ref, m_even)
        m_next = scores(h + 2, sa_ref)
        softmax_pv(h + 1, sb_ref, m_odd)
        return m_next

    m6 = lax.fori_loop(0, N_HEADS // 2 - 1, pair, scores(0, sa_ref))
    m7 = scores(N_HEADS - 1, sb_ref)
    softmax_pv(N_HEADS - 2, sa_ref, m6)
    softmax_pv(N_HEADS - 1, sb_ref, m7)


def _attention(q, k, vt):
    bsz, _, n, _ = q.shape
    n_all = k.shape[2]
    n_chunks = KV_CHUNKS
    assert n_all % (n_chunks * LANES) == 0
    return pl.pallas_call(
        functools.partial(_attn_kernel, n_chunks),
        grid=(bsz, n // Q_TILE),
        in_specs=[pl.BlockSpec((1, N_HEADS, Q_TILE, HEAD_PAD), lambda b, t: (b, 0, t, 0)),
                  pl.BlockSpec((1, N_HEADS, n_all, HEAD_PAD), lambda b, t: (b, 0, 0, 0)),
                  pl.BlockSpec((1, ATTN_W, n_all), lambda b, t: (b, 0, 0))],
        out_specs=pl.BlockSpec((1, ATTN_W, Q_TILE), lambda b, t: (b, 0, t)),
        out_shape=jax.ShapeDtypeStruct((bsz, ATTN_W, n), F32),
        scratch_shapes=[pltpu.VMEM((n_all, Q_TILE), F32), pltpu.VMEM((n_all, Q_TILE), F32)],
        compiler_params=_cparams(("arbitrary", "arbitrary")),
        name="attention",
    )(q, k, vt)


def _ssm_kernel(bsz, u0_ref, u1_ref, wb_ref, wc_ref, are_ref, aim_ref, j_ref, y0_ref, y1_ref,
                bua_ref, bub_ref, st_ref, yr_ref, xsa_ref, xsb_ref, ym_ref):
    i = pl.program_id(0)

    @pl.when(i == 0)
    def _():
        st_ref[...] = jnp.zeros_like(st_ref)

        xsa_ref[...] = jnp.zeros_like(xsa_ref)
        xsb_ref[...] = jnp.zeros_like(xsb_ref)

    sub = SSM_SUB
    n_blk = SSM_W // LANES
    jrev = j_ref[...]
    u0 = u0_ref[...].astype(BF16).astype(F32)
    u1 = u1_ref[...].astype(BF16)
    u1 = jnp.stack([jnp.dot(jrev, u1[b], preferred_element_type=F32) for b in range(bsz)])
    u_dir = (u0, u1)
    row_id = lax.broadcasted_iota(jnp.int32, (N_SEQ * sub, LANES), 0)
    is_dir1 = (row_id % N_SEQ) >= bsz

    def stage_inputs(xs_ref, s):
        for d in range(2):
            for b in range(bsz):
                rows = pl.ds(d * bsz + b, sub, stride=N_SEQ)
                for kb in range(n_blk):
                    xs_ref[kb, d, rows, :] = u_dir[d][b, s * sub:(s + 1) * sub, kb * LANES:(kb + 1) * LANES]

    def input_units(xs_ref, buf_ref):
        def unit(j):
            kb = j // 4
            x = jnp.concatenate([xs_ref[kb, 0], xs_ref[kb, 1]], axis=-1).astype(BF16)
            r = jnp.dot(x, wb_ref[j], preferred_element_type=F32)
            buf_ref[2 * j] = r[:, 0:LANES]
            buf_ref[2 * j + 1] = r[:, LANES:2 * LANES]
        return [functools.partial(unit, j) for j in range(N_PAIRS)]

    def readout_units(buf_ref, s):
        def unit(m):
            acc = None
            for j in range(4 * m, 4 * m + 4):
                st = jnp.concatenate([buf_ref[2 * j], buf_ref[2 * j + 1]], axis=-1).astype(BF16)
                r = jnp.dot(st, wc_ref[j], preferred_element_type=F32)
                t = jnp.where(is_dir1, r[:, LANES:2 * LANES], r[:, 0:LANES])
                acc = t if acc is None else acc + t
            ym_ref[m] = acc
            for d in range(2):
                dst = y0_ref if d == 0 else yr_ref
                for b in range(bsz):
                    dst[b, s * sub:(s + 1) * sub, m * LANES:(m + 1) * LANES] = \
                        ym_ref[m, pl.ds(d * bsz + b, sub, stride=N_SEQ), :]
        return [functools.partial(unit, m) for m in range(n_blk)]

    def scan(buf_ref, state, side_units):
        state = list(state)
        every = max(1, sub // max(1, len(side_units)))
        pending = list(side_units)
        for t in range(sub):
            for j in range(N_PAIRS):
                sre, sim = state[2 * j], state[2 * j + 1]
                are, aim = are_ref[j], aim_ref[j]
                rows = slice(t * N_SEQ, (t + 1) * N_SEQ)
                bre = buf_ref[2 * j, rows, :]
                bim = buf_ref[2 * j + 1, rows, :]
                nre = are * sre - aim * sim + bre
                nim = are * sim + aim * sre + bim
                buf_ref[2 * j, rows, :] = nre
                buf_ref[2 * j + 1, rows, :] = nim
                state[2 * j], state[2 * j + 1] = nre, nim
            if pending and t % every == 0:
                pending.pop(0)()
        for unit in pending:
            unit()
        return state

    state = [st_ref[k] for k in range(2 * N_PAIRS)]
    stage_inputs(xsa_ref, 0)
    stage_inputs(xsb_ref, 1)
    for unit in input_units(xsa_ref, bua_ref):
        unit()
    state = scan(bua_ref, state, input_units(xsb_ref, bub_ref))
    state = scan(bub_ref, state, readout_units(bua_ref, 0))
    for k in range(2 * N_PAIRS):
        st_ref[k] = state[k]
    for unit in readout_units(bub_ref, 1):
        unit()
    for b in range(bsz):
        y = yr_ref[b]
        hi = y.astype(BF16)
        lo = (y - hi.astype(F32)).astype(BF16)
        y1_ref[b] = (jnp.dot(jrev, hi, preferred_element_type=F32)
                     + jnp.dot(jrev, lo, preferred_element_type=F32))


def _ssm(u_all, ssm):
    bsz, n_all, _ = u_all.shape
    assert N_SEQ == 2 * bsz
    nc = n_all // SSM_CHUNK
    ctx_chunks = 256 // SSM_CHUNK

    def rev_map(i):
        return jnp.where(i < ctx_chunks, ctx_chunks - 1 - i, nc - 1 - (i - ctx_chunks))

    blk = (bsz, SSM_CHUNK, SSM_W)
    return pl.pallas_call(
        functools.partial(_ssm_kernel, bsz),
        grid=(nc,),
        in_specs=[pl.BlockSpec(blk, lambda i: (0, i, 0)),
                  pl.BlockSpec(blk, lambda i: (0, rev_map(i), 0)),
                  _full(ssm["wb"].shape), _full(ssm["wc"].shape),
                  _full(ssm["a_re"].shape), _full(ssm["a_im"].shape), _full(ssm["jrev"].shape)],
        out_specs=[pl.BlockSpec(blk, lambda i: (0, i, 0)),
                   pl.BlockSpec(blk, lambda i: (0, rev_map(i), 0))],
        out_shape=[jax.ShapeDtypeStruct(u_all.shape, F32)] * 2,
        scratch_shapes=[pltpu.VMEM((2 * N_PAIRS, N_SEQ * SSM_SUB, LANES), F32),
                        pltpu.VMEM((2 * N_PAIRS, N_SEQ * SSM_SUB, LANES), F32),
                        pltpu.VMEM((2 * N_PAIRS, N_SEQ, LANES), F32),
                        pltpu.VMEM(blk, F32),
                        pltpu.VMEM((SSM_W // LANES, 2, N_SEQ * SSM_SUB, LANES), F32),
                        pltpu.VMEM((SSM_W // LANES, 2, N_SEQ * SSM_SUB, LANES), F32),
                        pltpu.VMEM((SSM_W // LANES, N_SEQ * SSM_SUB, LANES), F32)],
        compiler_params=_cparams(("arbitrary",)),
        name="ssm",
    )(u_all, u_all, ssm["wb"], ssm["wc"], ssm["a_re"], ssm["a_im"], ssm["jrev"])


def _layer_norm(v, g, b):
    mu = jnp.mean(v, -1, keepdims=True)
    var = jnp.mean(jnp.square(v - mu), -1, keepdims=True)
    return (v - mu) * lax.rsqrt(var + EPS) * g + b


def _gelu_tanh(v):
    return 0.5 * v * (1.0 + jnp.tanh(math.sqrt(2.0 / math.pi) * (v + 0.044715 * (v * v * v))))


def _route(logits):
    lane = lax.broadcasted_iota(jnp.int32, logits.shape, 1).astype(F32)
    neg = jnp.float32(-jnp.inf)
    big = jnp.float32(1 << 20)
    gl = jnp.where((lane >= N_EXPERTS) & (lane < N_EXPERTS + N_EGROUPS), logits, neg)
    gmax = jnp.max(gl, -1, keepdims=True)
    g_w = 1.0 / jnp.sum(jnp.exp(gl - gmax), -1, keepdims=True)
    g_lane = jnp.min(jnp.where(gl == gmax, lane, big), -1, keepdims=True)
    lo = (g_lane - N_EXPERTS) * E_PER_GROUP
    el = jnp.where((lane >= lo) & (lane < lo + E_PER_GROUP), logits, neg)
    m1 = jnp.max(el, -1, keepdims=True)
    i1 = jnp.min(jnp.where(el == m1, lane, big), -1, keepdims=True)
    el2 = jnp.where(lane == i1, neg, el)
    m2 = jnp.max(el2, -1, keepdims=True)
    i2 = jnp.min(jnp.where(el2 == m2, lane, big), -1, keepdims=True)
    esum = jnp.sum(jnp.exp(el - m1), -1, keepdims=True)
    p1 = 1.0 / esum
    p2 = jnp.exp(m2 - m1) / esum
    w1 = p1 / (p1 + p2) * g_w
    w2 = p2 / (p1 + p2) * g_w
    onehot = jnp.where((lane == i1) | (lane == i2 + N_EXPERTS), 1.0, 0.0)
    info = jnp.where(lane == 0, i1, jnp.where(lane == 1, i2, jnp.where(lane == 2, w1, jnp.where(lane == 3, w2, 0.0))))
    return onehot, info


def _merge_kernel(at_ref, y0_ref, y1_ref, u_ref, x_ref, mod_ref, dsk_ref, wglu_ref, bglu_ref, gna_ref, gns_ref,
                  wo_ref, ln1g_ref, ln1b_ref, wrh_ref, wrl_ref, br_ref, x1_ref, hp_ref, oh_ref, info_ref, cnt_ref,
                  hs_ref):
    b = pl.program_id(0)
    mod = mod_ref[pl.ds(b, 1), :]
    gate_a = mod[:, 2 * D_MODEL:3 * D_MODEL]
    shift_f = mod[:, 3 * D_MODEL:4 * D_MODEL]
    scale_f = mod[:, 4 * D_MODEL:5 * D_MODEL]

    g = _gelu_tanh(y0_ref[0] + y1_ref[0] + dsk_ref[...] * u_ref[0])
    z = jnp.dot(g.astype(BF16), wglu_ref[...], preferred_element_type=F32) + bglu_ref[...]
    ssm = g * jax.nn.sigmoid(z)
    attn = at_ref[0].T
    an = _rms(attn, gna_ref[...]).astype(BF16)
    sn = _rms(ssm, gns_ref[...]).astype(BF16)
    mix = (jnp.dot(an, wo_ref[0:ATTN_W, :], preferred_element_type=F32)
           + jnp.dot(sn, wo_ref[ATTN_W:ATTN_W + SSM_W, :], preferred_element_type=F32))
    x1 = _layer_norm(ALPHA * x_ref[0] + gate_a * mix, ln1g_ref[...], ln1b_ref[...])
    x1_ref[0] = x1
    h = x1 * (1.0 + scale_f) + shift_f
    h_hi = h.astype(BF16)
    h_hif = h_hi.astype(F32)
    h_lo = (h - h_hif).astype(BF16)
    logits = (jnp.dot(h_hi, wrh_ref[...], preferred_element_type=F32)
              + jnp.dot(h_lo, wrh_ref[...], preferred_element_type=F32)
              + jnp.dot(h_hi, wrl_ref[...], preferred_element_type=F32)) + br_ref[...]
    onehot, info = _route(logits)
    oh_ref[...] = onehot.astype(BF16)
    info_ref[...] = info

    @pl.when((pl.program_id(0) == 0) & (pl.program_id(1) == 0))
    def _():
        cnt_ref[...] = jnp.zeros_like(cnt_ref)
    cnt_ref[...] += jnp.sum(onehot, 0, keepdims=True)
    t = h.shape[0]
    for c in range(SLAB_ROWS):
        hs_ref[pl.ds(c, t, stride=SLAB_ROWS), :] = h[:, c * LANES:(c + 1) * LANES]
    hp_ref[...] = hs_ref[...].astype(BF16)


def _merge(attn_t, y0, y1, u_all, x, mod, prep, ctx_len):
    bsz, n, _ = x.shape
    off = ctx_len // ROW_TILE
    tok = lambda w: pl.BlockSpec((1, ROW_TILE, w), lambda b, t: (b, t, 0))
    tok_off = lambda w: pl.BlockSpec((1, ROW_TILE, w), lambda b, t: (b, t + off, 0))
    weights = [prep["d_skip"], prep["w_glu"], prep["b_glu"], prep["gn_attn"], prep["gn_ssm"], prep["w_o"],
               prep["ln1_g"], prep["ln1_b"], prep["w_router_hi"], prep["w_router_lo"], prep["b_router"]]
    nt = n // ROW_TILE
    flat = lambda rows: pl.BlockSpec((rows, LANES), lambda b, t: (b * nt + t, 0))
    return pl.pallas_call(
        _merge_kernel,
        grid=(bsz, nt),
        in_specs=[pl.BlockSpec((1, ATTN_W, ROW_TILE), lambda b, t: (b, 0, t)),
                  tok_off(SSM_W), tok_off(SSM_W), tok_off(SSM_W), tok(D_MODEL), _full(mod.shape)]
                 + [_full(w.shape) for w in weights],
        out_specs=[tok(D_MODEL), flat(ROW_TILE * SLAB_ROWS), flat(ROW_TILE), flat(ROW_TILE), _full((8, LANES))],
        out_shape=[jax.ShapeDtypeStruct((bsz, n, D_MODEL), F32),
                   jax.ShapeDtypeStruct((bsz * n * SLAB_ROWS, LANES), BF16),
                   jax.ShapeDtypeStruct((bsz * n, LANES), BF16),
                   jax.ShapeDtypeStruct((bsz * n, LANES), F32),
                   jax.ShapeDtypeStruct((8, LANES), F32)],
        scratch_shapes=[pltpu.VMEM((ROW_TILE * SLAB_ROWS, LANES), F32)],
        compiler_params=_cparams(("arbitrary", "arbitrary")),
        name="merge",
    )(attn_t, y0, y1, u_all, x, mod, *weights)


def _rank_kernel(oh_ref, tri_ref, upper_ref, cnt_ref, pos_ref, carry_ref, base_ref):
    lane1 = lax.broadcasted_iota(jnp.int32, (8, LANES), 1)

    @pl.when(pl.program_id(0) == 0)
    def _():
        carry_ref[...] = jnp.zeros_like(carry_ref)
        c = cnt_ref[...]
        first = jnp.where(lane1 < N_EXPERTS, c, 0.0)
        second = pltpu.roll(c, LANES - N_EXPERTS, axis=1)
        tot = jnp.where(lane1 < N_EXPERTS, c + second, 0.0)
        tiles = jnp.right_shift(tot.astype(jnp.int32) + (EXPERT_TILE - 1), EXPERT_TILE_LOG2).astype(F32)
        start = EXPERT_TILE * jnp.dot(tiles.astype(BF16), upper_ref[...], preferred_element_type=F32)
        start = jnp.where(lane1 < N_EXPERTS, start, 0.0)
        base_ref[...] = start + pltpu.roll(start + first, N_EXPERTS, axis=1)

    oh = oh_ref[...]
    ohf = oh.astype(F32)
    before = jnp.dot(tri_ref[...], oh, preferred_element_type=F32) + carry_ref[0:1, :] + base_ref[0:1, :]
    lane = lax.broadcasted_iota(jnp.int32, ohf.shape, 1)
    hit = ohf * before
    p1 = jnp.sum(jnp.where(lane < N_EXPERTS, hit, 0.0), -1, keepdims=True)
    p2 = jnp.sum(jnp.where(lane >= N_EXPERTS, hit, 0.0), -1, keepdims=True)
    pos_ref[...] = jnp.where(lane == 0, p1, jnp.where(lane == 1, p2, 0.0))
    carry_ref[...] = carry_ref[...] + jnp.sum(ohf, 0, keepdims=True)


def _rank(onehot, counts):
    rows = onehot.shape[0]
    tile = min(RANK_TILE, rows)
    tri = jnp.tril(jnp.ones((tile, tile), F32), -1).astype(BF16)
    upper = jnp.triu(jnp.ones((LANES, LANES), F32), 1).astype(BF16)
    return pl.pallas_call(
        _rank_kernel,
        grid=(rows // tile,),
        in_specs=[pl.BlockSpec((tile, LANES), lambda i: (i, 0)), _full(tri.shape), _full(upper.shape),
                  _full(counts.shape)],
        out_specs=pl.BlockSpec((tile, LANES), lambda i: (i, 0)),
        out_shape=jax.ShapeDtypeStruct((rows, LANES), F32),
        scratch_shapes=[pltpu.VMEM((8, LANES), F32), pltpu.VMEM((8, LANES), F32)],
        compiler_params=_cparams(("arbitrary",)),
        name="rank",
    )(onehot, tri, upper, counts)


def _expert_kernel(n_tok, te_ref, tn_ref, src_ref, hp_hbm, wg_ref, wu_ref, wd_ref, y_hbm,
                   hres_ref, xbuf_ref, yslab_ref, pend_ref, load_sem, out_sem):
    i = pl.program_id(0)
    last = pl.num_programs(0) - 1
    slot = i % 2
    tm = EXPERT_TILE
    n_valid = tn_ref[i]

    def wait_rows(s):
        rows = pend_ref[s] * SLAB_ROWS
        pltpu.make_async_copy(yslab_ref.at[s, pl.ds(0, rows), :], y_hbm.at[pl.ds(0, rows), :], out_sem.at[s]).wait()
        pend_ref[s] = 0

    def row_copy(r, a):
        return pltpu.make_async_copy(yslab_ref.at[slot, pl.ds(pl.multiple_of(r * SLAB_ROWS, SLAB_ROWS), SLAB_ROWS), :],
                                     y_hbm.at[pl.ds(pl.multiple_of(a * SLAB_ROWS, SLAB_ROWS), SLAB_ROWS), :],
                                     out_sem.at[slot])

    @pl.when(i == 0)
    def _():
        pend_ref[0] = 0
        pend_ref[1] = 0
        cp = pltpu.make_async_copy(hp_hbm, hres_ref, load_sem)
        cp.start()
        cp.wait()

    @pl.when(n_valid > 0)
    def _():
        @pl.when(pend_ref[slot] > 0)
        def _():
            wait_rows(slot)

        for r in range(tm):
            tok = src_ref[0, 0, r] & (n_tok - 1)
            pair = hres_ref[tok >> 1].astype(F32)
            xbuf_ref[pl.ds(r, SLAB_ROWS, stride=XBUF_PITCH), :] = jnp.where(
                (tok & 1) == 1, pair[SLAB_ROWS:2 * SLAB_ROWS], pair[0:SLAB_ROWS])
        x = jnp.concatenate([xbuf_ref[c * XBUF_PITCH:c * XBUF_PITCH + tm, :] for c in range(SLAB_ROWS)],
                            axis=-1).astype(BF16)
        a = jnp.dot(x, wg_ref[0].astype(BF16), preferred_element_type=F32)
        u = jnp.dot(x, wu_ref[0].astype(BF16), preferred_element_type=F32)
        act = (a * jax.nn.sigmoid(a) * u).astype(BF16)
        y = jnp.dot(act, wd_ref[0].astype(BF16), preferred_element_type=F32)
        for c in range(SLAB_ROWS):
            yslab_ref[slot, pl.ds(c, tm, stride=SLAB_ROWS), :] = y[:, c * LANES:(c + 1) * LANES]

        @pl.when(n_valid == tm)
        def _():
            for r in range(tm):
                row_copy(r, src_ref[0, 0, r]).start()

        @pl.when(n_valid < tm)
        def _():
            def issue(r, carry):
                row_copy(r, src_ref[0, 0, r]).start()
                return carry
            lax.fori_loop(0, n_valid, issue, 0)

        pend_ref[slot] = n_valid

    @pl.when(i == last)
    def _():
        for s in range(2):
            @pl.when(pend_ref[s] > 0)
            def _():
                wait_rows(s)


def _experts(tile_e, tile_n, src, hp, wg, wu, wd, n_tok):
    n_tiles = tile_e.shape[0]
    tm = EXPERT_TILE
    assert n_tok & (n_tok - 1) == 0
    hp = hp.reshape(n_tok // 2, 2 * SLAB_ROWS, LANES)
    wspec = lambda shape: pl.BlockSpec((1,) + shape, lambda i, te, tn: (te[i], 0, 0))
    grid_spec = pltpu.PrefetchScalarGridSpec(
        num_scalar_prefetch=2,
        grid=(n_tiles,),
        in_specs=[pl.BlockSpec((1, 1, tm), lambda i, te, tn: (i, 0, 0), memory_space=pltpu.SMEM),
                  pl.BlockSpec(memory_space=pl.ANY),
                  wspec((D_MODEL, D_FF)), wspec((D_MODEL, D_FF)), wspec((D_FF, D_MODEL))],
        out_specs=pl.BlockSpec(memory_space=pl.ANY),
        scratch_shapes=[pltpu.VMEM(hp.shape, BF16),
                        pltpu.VMEM((SLAB_ROWS * XBUF_PITCH, LANES), F32),
                        pltpu.VMEM((2, tm * SLAB_ROWS, LANES), F32),
                        pltpu.SMEM((2,), jnp.int32),
                        pltpu.SemaphoreType.DMA(()),
                        pltpu.SemaphoreType.DMA((2,))])
    return pl.pallas_call(
        functools.partial(_expert_kernel, n_tok),
        grid_spec=grid_spec,
        out_shape=jax.ShapeDtypeStruct((2 * n_tok * SLAB_ROWS, LANES), F32),
        compiler_params=pltpu.CompilerParams(dimension_semantics=("arbitrary",),
                                             vmem_limit_bytes=EXPERT_VMEM_LIMIT),
        name="experts",
    )(tile_e, tile_n, src.reshape(n_tiles, 1, tm), hp, wg, wu, wd)


def _combine_kernel(y0_ref, y1_ref, info_ref, x1_ref, mod_ref, ln2g_ref, ln2b_ref, o_ref):
    b = pl.program_id(0)
    gate_f = mod_ref[pl.ds(b, 1), 5 * D_MODEL:6 * D_MODEL]
    t = x1_ref.shape[1]
    info = info_ref[...]
    w1, w2 = info[:, 2:3], info[:, 3:4]
    ffn = jnp.concatenate([w1 * y0_ref[pl.ds(c, t, stride=SLAB_ROWS), :] + w2 * y1_ref[pl.ds(c, t, stride=SLAB_ROWS), :]
                           for c in range(SLAB_ROWS)], axis=-1)
    o_ref[0] = _layer_norm(ALPHA * x1_ref[0] + gate_f * ffn, ln2g_ref[...], ln2b_ref[...])


def _combine(y_slots, info, x1, mod, ln2_g, ln2_b):
    bsz, n, _ = x1.shape
    nt = n // ROW_TILE
    slab = lambda k: pl.BlockSpec((ROW_TILE * SLAB_ROWS, LANES), lambda b, t: (k * bsz * nt + b * nt + t, 0))
    return pl.pallas_call(
        _combine_kernel,
        grid=(bsz, nt),
        in_specs=[slab(0), slab(1), pl.BlockSpec((ROW_TILE, LANES), lambda b, t: (b * nt + t, 0)),
                  pl.BlockSpec((1, ROW_TILE, D_MODEL), lambda b, t: (b, t, 0)),
                  _full(mod.shape), _full(ln2_g.shape), _full(ln2_b.shape)],
        out_specs=pl.BlockSpec((1, ROW_TILE, D_MODEL), lambda b, t: (b, t, 0)),
        out_shape=jax.ShapeDtypeStruct((bsz, n, D_MODEL), F32),
        compiler_params=_cparams(("arbitrary", "arbitrary")),
        name="combine",
    )(y_slots, y_slots, info, x1, mod, ln2_g, ln2_b)


def _invert_kernel(n_tok, n_tiles, cnt_ref, p1_ref, p2_ref, src_ref, te_ref, tn_ref):
    step = pl.program_id(0)
    tm = EXPERT_TILE
    chunk = p1_ref.shape[2]

    @pl.when(step == 0)
    def _():
        def fill(p, c):
            src_ref[p] = 0
            return c

        def per_expert(e, carry):
            tile0, last_e = carry
            cnt = cnt_ref[e] + cnt_ref[N_EXPERTS + e]
            nt = (cnt + (tm - 1)) >> EXPERT_TILE_LOG2

            def per_tile(j, c):
                te_ref[tile0 + j] = e
                tn_ref[tile0 + j] = jnp.minimum(cnt - j * tm, tm)
                return c
            lax.fori_loop(0, nt, per_tile, 0)
            lax.fori_loop(tile0 * tm + cnt, (tile0 + nt) * tm, fill, 0)
            return tile0 + nt, jnp.where(nt > 0, e, last_e)

        used, last_e = lax.fori_loop(0, N_EXPERTS, per_expert, (jnp.int32(0), jnp.int32(0)))

        def idle(j, c):
            te_ref[j] = last_e
            tn_ref[j] = 0
            return c
        lax.fori_loop(used, n_tiles, idle, 0)
        lax.fori_loop(used * tm, n_tiles * tm, fill, 0)

    def body(j, c):
        t = step * chunk + j
        src_ref[p1_ref[0, 0, j]] = t
        src_ref[p2_ref[0, 0, j]] = n_tok + t
        return c
    lax.fori_loop(0, chunk, body, 0, unroll=8)


def _invert(pos, counts, n_tok):
    tm = EXPERT_TILE
    n_tiles = 2 * n_tok // tm + N_EXPERTS
    chunk = min(INVERT_CHUNK, n_tok)
    nc = n_tok // chunk
    col = lambda a, j: a[:, j].astype(jnp.int32).reshape(nc, 1, chunk)
    cnt = counts[0, 0:2 * N_EXPERTS].astype(jnp.int32)
    idx_spec = pl.BlockSpec((1, 1, chunk), lambda i, c: (i, 0, 0), memory_space=pltpu.SMEM)
    smem_out = pl.BlockSpec(memory_space=pltpu.SMEM)
    grid_spec = pltpu.PrefetchScalarGridSpec(
        num_scalar_prefetch=1, grid=(nc,),
        in_specs=[idx_spec] * 2,
        out_specs=[smem_out] * 3)
    return pl.pallas_call(
        functools.partial(_invert_kernel, n_tok, n_tiles),
        grid_spec=grid_spec,
        out_shape=[jax.ShapeDtypeStruct((n_tiles * tm,), jnp.int32),
                   jax.ShapeDtypeStruct((n_tiles,), jnp.int32),
                   jax.ShapeDtypeStruct((n_tiles,), jnp.int32)],
        compiler_params=_cparams(("arbitrary",)),
        name="invert",
    )(cnt, col(pos, 0), col(pos, 1))


def _rope_tables(n):
    rows = n // GRID_W
    row = jnp.repeat(jnp.arange(rows), GRID_W).astype(F32)
    col = jnp.tile(jnp.arange(GRID_W), rows).astype(F32)
    axis_dim = QK_ROPE // 2
    inv_freq = ROPE_BASE ** (-jnp.arange(0, axis_dim, 2, dtype=F32) / axis_dim)
    ang = jnp.concatenate([row[:, None] * inv_freq, col[:, None] * inv_freq], -1)
    ang = jnp.concatenate([ang, ang], -1)
    pad_l = jnp.ones((n, QK_NOPE), F32)
    pad_r = jnp.ones((n, HEAD_PAD - QK_NOPE - QK_ROPE), F32)
    cos_t = jnp.concatenate([pad_l, jnp.cos(ang), pad_r], -1)
    sin_t = jnp.concatenate([0 * pad_l, jnp.sin(ang), 0 * pad_r], -1)
    return cos_t, sin_t


def _rot_half_cols(w):
    w1, w2 = w[..., :QK_ROPE // 2], w[..., QK_ROPE // 2:]
    return jnp.concatenate([-w2, w1], -1)


def _prep_project(w_in, q_norm_g, kv_norm_g, w_uq, w_ukv):
    zeros = lambda r, c: jnp.zeros((r, c), F32)
    w_in_p = jnp.concatenate([w_in[:, :MLA_IN], zeros(D_MODEL, 512 - MLA_IN), w_in[:, MLA_IN:]], -1)
    qk_scale = SM_SCALE * math.log2(math.e)
    wq = (w_uq * qk_scale).reshape(Q_RANK, N_HEADS, QK_NOPE + QK_ROPE)
    pad = jnp.zeros((Q_RANK, N_HEADS, HEAD_PAD - QK_NOPE - QK_ROPE), F32)
    wq_pad = jnp.concatenate([wq, pad], -1).reshape(Q_RANK, N_HEADS * HEAD_PAD)
    wq_rot = jnp.concatenate([0 * wq[..., :QK_NOPE], _rot_half_cols(wq[..., QK_NOPE:]), pad], -1)
    wq_rot = wq_rot.reshape(Q_RANK, N_HEADS * HEAD_PAD)
    wkv = w_ukv.reshape(KV_RANK, N_HEADS, QK_NOPE + V_DIM)
    wk = jnp.concatenate([wkv[..., :QK_NOPE], jnp.zeros((KV_RANK, N_HEADS, HEAD_PAD - QK_NOPE), F32)], -1)
    wk = wk.reshape(KV_RANK, N_HEADS * HEAD_PAD)
    wvt = wkv[..., QK_NOPE:].reshape(KV_RANK, ATTN_W).T
    eye = jnp.eye(QK_ROPE, dtype=F32)
    place = lambda m: jnp.zeros((LANES, HEAD_PAD), F32).at[:QK_ROPE, QK_NOPE:QK_NOPE + QK_ROPE].set(m)
    return dict(w_in=w_in_p.astype(BF16), q_g=q_norm_g.reshape(1, -1), kv_g=kv_norm_g.reshape(1, -1),
                wq=wq_pad.astype(BF16), wq_rot=wq_rot.astype(BF16), wk=wk.astype(BF16),
                wvt=wvt.astype(BF16), e=place(eye).astype(BF16), e_rot=place(_rot_half_cols(eye)).astype(BF16))


def _discretise(a_re, a_im, log_dt, b_re, b_im):
    dt = jnp.exp(log_dt)[:, None]
    mag = jnp.exp(a_re * dt)
    abar_re, abar_im = mag * jnp.cos(a_im * dt), mag * jnp.sin(a_im * dt)
    den = jnp.square(a_re) + jnp.square(a_im)
    num_re = abar_re - 1.0
    coef_re = (num_re * a_re + abar_im * a_im) / den
    coef_im = (abar_im * a_re - num_re * a_im) / den
    bbar_re = coef_re[..., None] * b_re - coef_im[..., None] * b_im
    bbar_im = coef_re[..., None] * b_im + coef_im[..., None] * b_re
    return abar_re, abar_im, bbar_re, bbar_im


def _prep_ssm(a_re, a_im, log_dt, b_re, b_im, c_re, c_im, bsz):
    wbs, wcs, ares, aims = [], [], [], []
    eye2 = jnp.eye(2, dtype=F32)
    quarter = jax.nn.one_hot(jnp.arange(N_PAIRS) % 4, 4, dtype=F32)
    for d in range(2):
        abar_re, abar_im, bbar_re, bbar_im = _discretise(a_re[d], a_im[d], log_dt[d], b_re[d], b_im[d])

        def in_block(bb):
            t = bb.reshape(N_PAIRS, 2, SSM_STATE, SSM_GROUP)
            return jnp.einsum('jgph,gk->jghkp', t, eye2).reshape(N_PAIRS, 2 * SSM_GROUP, 2 * SSM_STATE)

        blk = jnp.concatenate([in_block(bbar_re), in_block(bbar_im)], -1)
        wbs.append(jnp.einsum('jrc,jq->jqrc', blk, quarter).reshape(N_PAIRS, LANES, 2 * LANES))

        def out_block(cc):
            t = cc.reshape(N_PAIRS, 2, SSM_GROUP, SSM_STATE)
            return jnp.einsum('jghp,gk->jgpkh', t, eye2).reshape(N_PAIRS, 2 * SSM_STATE, 2 * SSM_GROUP)

        oblk = jnp.concatenate([out_block(c_re[d]), -out_block(c_im[d])], 1)
        wcs.append(jnp.einsum('jkc,jq->jkqc', oblk, quarter).reshape(N_PAIRS, 2 * LANES, LANES))
        ares.append(jnp.broadcast_to(abar_re.reshape(N_PAIRS, 1, LANES), (N_PAIRS, bsz, LANES)))
        aims.append(jnp.broadcast_to(abar_im.reshape(N_PAIRS, 1, LANES), (N_PAIRS, bsz, LANES)))
    jrev = jnp.eye(SSM_CHUNK, dtype=F32)[::-1]
    return dict(wb=jnp.concatenate(wbs, axis=1).astype(BF16), wc=jnp.concatenate(wcs, axis=2).astype(BF16),
                a_re=jnp.concatenate(ares, 1), a_im=jnp.concatenate(aims, 1), jrev=jrev.astype(BF16))


def kernel(x, c, ctx, c_ctx, w_ada, b_ada, w_in, q_norm_g, kv_norm_g, w_uq, w_ukv, ssm_a_re, ssm_a_im,
           ssm_log_dt, ssm_b_re, ssm_b_im, ssm_c_re, ssm_c_im, ssm_d, w_glu, b_glu, gn_attn_g, gn_ssm_g, w_o,
           ln1_g, ln1_b, w_router_group, b_router_group, w_router_expert, b_router_expert, w_exp_gate,
           w_exp_up, w_exp_down, ln2_g, ln2_b):
    bsz, n, _ = x.shape
    ctx_len = ctx.shape[1]
    n_all = ctx_len + n
    l = 0
    row = lambda v: v.reshape(1, -1)

    cvec = jnp.concatenate([c, c_ctx[None, :], jnp.zeros((8 - bsz - 1, D_MODEL), F32)], 0)
    mod = _adaln(cvec, w_ada[l], b_ada[l])

    prep = _prep_project(w_in[l], q_norm_g[l], kv_norm_g[l], w_uq[l], w_ukv[l])
    cos_t, sin_t = _rope_tables(n)
    ones_t = jnp.ones((ctx_len, HEAD_PAD), F32)
    q, k, vt, u_all = _project(x, mod, cos_t, sin_t, prep, n_all, ctx_len)
    k, vt, u_all = _project(ctx, mod, ones_t, 0 * ones_t, prep, n_all, 0, prev=(k, vt, u_all))

    attn_t = _attention(q, k, vt)

    ssm = _prep_ssm(ssm_a_re[l], ssm_a_im[l], ssm_log_dt[l], ssm_b_re[l], ssm_b_im[l],
                    ssm_c_re[l], ssm_c_im[l], bsz)
    y0, y1 = _ssm(u_all, ssm)

    w_router = jnp.concatenate([w_router_expert[l], w_router_group[l],
                                jnp.zeros((D_MODEL, LANES - N_EXPERTS - N_EGROUPS), F32)], -1)
    b_router = jnp.concatenate([b_router_expert[l], b_router_group[l],
                                jnp.zeros((LANES - N_EXPERTS - N_EGROUPS,), F32)])
    w_router_hi = w_router.astype(BF16)
    mprep = dict(d_skip=row(ssm_d[l]), w_glu=w_glu[l].astype(BF16), b_glu=row(b_glu[l]),
                 gn_attn=row(gn_attn_g[l]), gn_ssm=row(gn_ssm_g[l]), w_o=w_o[l].astype(BF16),
                 ln1_g=row(ln1_g[l]), ln1_b=row(ln1_b[l]), w_router_hi=w_router_hi,
                 w_router_lo=(w_router - w_router_hi.astype(F32)).astype(BF16), b_router=row(b_router))
    x1, hp, onehot, info, counts = _merge(attn_t, y0, y1, u_all, x, mod, mprep, ctx_len)

    pos = _rank(onehot, counts)
    src, tile_e, tile_n = _invert(pos, counts, bsz * n)
    y_slots = _experts(tile_e, tile_n, src, hp, w_exp_gate[l], w_exp_up[l], w_exp_down[l], bsz * n)
    return _combine(y_slots, info, x1, mod, row(ln2_g[l]), row(ln2_b[l]))
```
